```python
import math
import jax, jax.numpy as jnp
from jax import lax
import numpy as np

D_MODEL = 2048
BATCH = 1
SEQ = 8192
DEPTH = 2
DEC_BATCH = 2
DEC_SEQ = 4096
PAST_LEN = 128

HEAD_DIM = 128
A_HEADS = 8
A_WIDTH = A_HEADS * HEAD_DIM
DILATED_PATTERNS = ((128, 1), (512, 4), (2048, 16))
A_QBLOCK = 64
B_HEADS = 8
B_KV_HEADS = 2
B_GROUPS = B_HEADS // B_KV_HEADS
B_WIDTH = B_HEADS * HEAD_DIM
B_KV_WIDTH = B_KV_HEADS * HEAD_DIM
B_QBLOCK = 128
GRID_W = 64
ROPE_THETA = 10000.0
N_BUCKETS = 32
MAX_DISTANCE = 1024
IN_COLS = 3 * A_WIDTH + B_WIDTH + 2 * B_KV_WIDTH + 2 * D_MODEL
D_FF = 5632
N_EXPERTS = 8
TOP_K = 2
D_FF_EXPERT = 7168
N_DENSE = (DEPTH + 1) // 2
N_MOE = DEPTH // 2
DEEPNORM_ALPHA = (2.0 * DEPTH) ** 0.25
DEEPNORM_BETA = (8.0 * DEPTH) ** -0.25
LN_EPS = 1e-5
RMS_EPS = 1e-6
NEG_INF = -1e30

kernel_name = "hybrid_dilated_gqa_deepnorm_encoder"


def layer_norm(x, g, b):
    xf = x.astype(jnp.float32)
    mu = jnp.mean(xf, axis=-1, keepdims=True)
    var = jnp.mean(jnp.square(xf - mu), axis=-1, keepdims=True)
    y = (xf - mu) * lax.rsqrt(var + LN_EPS) * g.astype(jnp.float32) + b.astype(jnp.float32)
    return y.astype(x.dtype)


def rms_norm_heads(x, g):
    xf = x.astype(jnp.float32)
    y = xf * lax.rsqrt(jnp.mean(jnp.square(xf), axis=-1, keepdims=True) + RMS_EPS) * g.astype(jnp.float32)
    return y.astype(x.dtype)


def t5_bucket(rel):
    nb = N_BUCKETS // 2
    max_exact = nb // 2
    n = jnp.abs(rel)
    large = max_exact + (jnp.log(jnp.maximum(n, 1).astype(jnp.float32) / max_exact)
                         / math.log(MAX_DISTANCE / max_exact) * (nb - max_exact)).astype(jnp.int32)
    large = jnp.minimum(large, nb - 1)
    return jnp.where(rel > 0, nb, 0) + jnp.where(n < max_exact, n, large)


def axial_rope_tables(seq_len, dtype):
    rows = seq_len // GRID_W
    row = jnp.repeat(jnp.arange(rows), GRID_W).astype(jnp.float32)
    col = jnp.tile(jnp.arange(GRID_W), rows).astype(jnp.float32)
    n_freq = HEAD_DIM // 4
    inv_freq = ROPE_THETA ** (-jnp.arange(n_freq, dtype=jnp.float32) / n_freq)
    ang_r = row[:, None] * inv_freq[None, :]
    ang_c = col[:, None] * inv_freq[None, :]
    ang = jnp.concatenate([ang_r, ang_r, ang_c, ang_c], axis=-1)
    return jnp.cos(ang).astype(dtype), jnp.sin(ang).astype(dtype)


def _rotate_half(x):
    h = x.shape[-1] // 2
    return jnp.concatenate([-x[..., h:], x[..., :h]], axis=-1)


def apply_axial_rope(x, cos, sin):
    half = HEAD_DIM // 2
    rot = jnp.concatenate([_rotate_half(x[..., :half]), _rotate_half(x[..., half:])], axis=-1)
    return x * cos[None, :, None, :] + rot * sin[None, :, None, :]


def dilated_window_attention(q, k, v, rel_bias):
    B, S, H, Dh = q.shape
    scale = Dh ** -0.5
    outs, lses = [], []
    for window, dil in DILATED_PATTERNS:
        half = window // (2 * dil)
        L = S // dil
        nb = -(-L // A_QBLOCK)
        Lp = nb * A_QBLOCK
        kw = A_QBLOCK + 2 * half

        def to_sub(t):
            return t.reshape(B, L, dil, H, Dh).transpose(0, 2, 1, 3, 4)

        qs = jnp.pad(to_sub(q), ((0, 0), (0, 0), (0, Lp - L), (0, 0), (0, 0)))
        qs = qs.reshape(B, dil, nb, A_QBLOCK, H, Dh)
        pad_kv = ((0, 0), (0, 0), (half, Lp - L + half), (0, 0), (0, 0))
        kpad = jnp.pad(to_sub(k), pad_kv)
        vpad = jnp.pad(to_sub(v), pad_kv)
        idx = jnp.arange(nb)[:, None] * A_QBLOCK + jnp.arange(kw)[None, :]
        kg = kpad[:, :, idx]
        vg = vpad[:, :, idx]

        logits = jnp.einsum('brnqhe,brnkhe->brnhqk', qs, kg).astype(jnp.float32) * scale
        rel = jnp.arange(kw)[None, :] - half - jnp.arange(A_QBLOCK)[:, None]
        key_l = idx - half
        valid = (jnp.abs(rel) <= half)[None, :, :] & ((key_l >= 0) & (key_l < L))[:, None, :]
        bias = rel_bias[t5_bucket(rel * dil)].astype(jnp.float32).transpose(2, 0, 1)
        logits = jnp.where(valid[None, None, :, None], logits + bias[None, None, None], NEG_INF)
        m = jnp.max(logits, axis=-1, keepdims=True)
        p = jnp.exp(logits - m)
        s = jnp.sum(p, axis=-1, keepdims=True)
        o = jnp.einsum('brnhqk,brnkhe->brnqhe', (p / s).astype(v.dtype), vg)
        o = o.reshape(B, dil, Lp, H, Dh)[:, :, :L].transpose(0, 2, 1, 3, 4).reshape(B, S, H, Dh)
        lse = (m + jnp.log(s))[..., 0]
        lse = lse.transpose(0, 1, 2, 4, 3).reshape(B, dil, Lp, H)[:, :, :L]
        lse = lse.transpose(0, 2, 1, 3).reshape(B, S, H)
        outs.append(o)
        lses.append(lse)
    w = jax.nn.softmax(jnp.stack(lses, axis=0), axis=0)
    o = jnp.einsum('pbsh,pbshe->bshe', w, jnp.stack(outs, axis=0).astype(jnp.float32))
    return o.astype(q.dtype)


def gqa_blocked_attention(q, k, v):
    B, S, _, Dh = q.shape
    nb = S // B_QBLOCK
    scale = Dh ** -0.5
    qb = q.reshape(B, nb, B_QBLOCK, B_KV_HEADS, B_GROUPS, Dh).transpose(1, 0, 3, 4, 2, 5)
    kt = k.transpose(0, 2, 1, 3)
    vt = v.transpose(0, 2, 1, 3)

    def one_block(qblk):
        s = jnp.einsum('bkgqe,bkse->bkgqs', qblk, kt).astype(jnp.float32) * scale
        p = jax.nn.softmax(s, axis=-1)
        return jnp.einsum('bkgqs,bkse->bkgqe', p.astype(vt.dtype), vt)

    o = lax.map(one_block, qb)
    return o.transpose(1, 0, 4, 2, 3, 5).reshape(B, S, B_HEADS * Dh)


def token_mixer(x, w_in_l, b_gate_l, q_g, k_g, rel_bias, w_ba, w_bb, w_o, cos, sin):
    B, S, _ = x.shape
    h = x @ w_in_l
    cuts = [A_WIDTH, 2 * A_WIDTH, 3 * A_WIDTH, 3 * A_WIDTH + B_WIDTH,
            3 * A_WIDTH + B_WIDTH + B_KV_WIDTH, 3 * A_WIDTH + B_WIDTH + 2 * B_KV_WIDTH,
            3 * A_WIDTH + B_WIDTH + 2 * B_KV_WIDTH + D_MODEL]
    qa, ka, va, qb, kb, vb, ga, gb = jnp.split(h, cuts, axis=-1)
    qa = qa.reshape(B, S, A_HEADS, HEAD_DIM)
    ka = ka.reshape(B, S, A_HEADS, HEAD_DIM)
    va = va.reshape(B, S, A_HEADS, HEAD_DIM)
    ya = dilated_window_attention(qa, ka, va, rel_bias).reshape(B, S, A_WIDTH) @ w_ba
    qb = apply_axial_rope(rms_norm_heads(qb.reshape(B, S, B_HEADS, HEAD_DIM), q_g), cos, sin)
    kb = apply_axial_rope(rms_norm_heads(kb.reshape(B, S, B_KV_HEADS, HEAD_DIM), k_g), cos, sin)
    vb = vb.reshape(B, S, B_KV_HEADS, HEAD_DIM)
    yb = gqa_blocked_attention(qb, kb, vb) @ w_bb
    merged = jax.nn.sigmoid(ga + b_gate_l[0]) * ya + jax.nn.sigmoid(gb + b_gate_l[1]) * yb
    return merged @ w_o


def swiglu(x, wg, wu, wd):
    return (jax.nn.silu(x @ wg) * (x @ wu)) @ wd


def moe_swiglu(x, w_router, wg, wu, wd):
    logits = (x @ w_router).astype(jnp.float32)
    top_val, top_idx = lax.top_k(logits, TOP_K)
    top_w = jax.nn.softmax(top_val, axis=-1)
    gate = jnp.sum(jax.nn.one_hot(top_idx, N_EXPERTS, dtype=jnp.float32) * top_w[..., None], axis=-2)
    out = jnp.zeros_like(x)
    for e in range(N_EXPERTS):
        out = out + gate[..., e:e + 1].astype(x.dtype) * swiglu(x, wg[e], wu[e], wd[e])
    return out


def encoder_trunk(x, w_in, b_gate, q_norm_g, k_norm_g, rel_bias, w_branch_a, w_branch_b, w_out,
                  ln1_g, ln1_b, ln2_g, ln2_b, ffn_w_gate, ffn_w_up, ffn_w_down,
                  router_w, exp_w_gate, exp_w_up, exp_w_down):
    cos, sin = axial_rope_tables(x.shape[1], x.dtype)
    for l in range(DEPTH):
        y = token_mixer(x, w_in[l], b_gate[l], q_norm_g[l], k_norm_g[l], rel_bias,
                        w_branch_a[l], w_branch_b[l], w_out[l], cos, sin)
        x = layer_norm(DEEPNORM_ALPHA * x + y, ln1_g[l], ln1_b[l])
        j = l // 2
        if l % 2 == 0:
            f = swiglu(x, ffn_w_gate[j], ffn_w_up[j], ffn_w_down[j])
        else:
            f = moe_swiglu(x, router_w[j], exp_w_gate[j], exp_w_up[j], exp_w_down[j])
        x = layer_norm(DEEPNORM_ALPHA * x + f, ln2_g[l], ln2_b[l])
    return x


def setup_inputs(seed: int = 0) -> dict:
    key = jax.random.key(seed)
    ks = jax.random.split(key, 21)
    f32 = jnp.float32

    def nrm(k, shape, scale):
        return jax.random.normal(k, shape, f32) * scale

    col_scale = np.ones((IN_COLS,), np.float32)
    col_scale[2 * A_WIDTH:3 * A_WIDTH] = DEEPNORM_BETA
    vb0 = 3 * A_WIDTH + B_WIDTH + B_KV_WIDTH
    col_scale[vb0:vb0 + B_KV_WIDTH] = DEEPNORM_BETA
    beta = DEEPNORM_BETA
    return {
        "x_prompt": nrm(ks[0], (BATCH, SEQ, D_MODEL), 1.0),
        "x_sample": nrm(ks[1], (DEC_BATCH, DEC_SEQ, D_MODEL), 1.0),
        "w_in": nrm(ks[2], (DEPTH, D_MODEL, IN_COLS), D_MODEL ** -0.5) * jnp.asarray(col_scale),
        "b_gate": nrm(ks[3], (DEPTH, 2, D_MODEL), 0.02),
        "q_norm_g": 1.0 + nrm(ks[4], (DEPTH, HEAD_DIM), 0.02),
        "k_norm_g": 1.0 + nrm(ks[5], (DEPTH, HEAD_DIM), 0.02),
        "rel_bias": nrm(ks[6], (N_BUCKETS, A_HEADS), 0.2),
        "w_branch_a": nrm(ks[7], (DEPTH, A_WIDTH, D_MODEL), A_WIDTH ** -0.5 * beta),
        "w_branch_b": nrm(ks[8], (DEPTH, B_WIDTH, D_MODEL), B_WIDTH ** -0.5 * beta),
        "w_out": nrm(ks[9], (DEPTH, D_MODEL, D_MODEL), D_MODEL ** -0.5 * beta),
        "ln1_g": 1.0 + nrm(ks[10], (DEPTH, D_MODEL), 0.02),
        "ln1_b": nrm(ks[11], (DEPTH, D_MODEL), 0.02),
        "ln2_g": 1.0 + nrm(ks[12], (DEPTH, D_MODEL), 0.02),
        "ln2_b": nrm(ks[13], (DEPTH, D_MODEL), 0.02),
        "ffn_w_gate": nrm(ks[14], (N_DENSE, D_MODEL, D_FF), D_MODEL ** -0.5),
        "ffn_w_up": nrm(ks[15], (N_DENSE, D_MODEL, D_FF), D_MODEL ** -0.5 * beta),
        "ffn_w_down": nrm(ks[16], (N_DENSE, D_FF, D_MODEL), D_FF ** -0.5 * beta),
        "router_w": nrm(ks[17], (N_MOE, D_MODEL, N_EXPERTS), D_MODEL ** -0.5),
        "exp_w_gate": nrm(ks[18], (N_MOE, N_EXPERTS, D_MODEL, D_FF_EXPERT), D_MODEL ** -0.5),
        "exp_w_up": nrm(ks[19], (N_MOE, N_EXPERTS, D_MODEL, D_FF_EXPERT), D_MODEL ** -0.5 * beta),
        "exp_w_down": nrm(ks[20], (N_MOE, N_EXPERTS, D_FF_EXPERT, D_MODEL), D_FF_EXPERT ** -0.5 * beta),
    }


def reference(x_prompt, x_sample, w_in, b_gate, q_norm_g, k_norm_g, rel_bias, w_branch_a, w_branch_b,
              w_out, ln1_g, ln1_b, ln2_g, ln2_b, ffn_w_gate, ffn_w_up, ffn_w_down,
              router_w, exp_w_gate, exp_w_up, exp_w_down):
    y_prompt = encoder_trunk(x_prompt, w_in, b_gate, q_norm_g, k_norm_g, rel_bias, w_branch_a, w_branch_b,
                             w_out, ln1_g, ln1_b, ln2_g, ln2_b, ffn_w_gate, ffn_w_up, ffn_w_down,
                             router_w, exp_w_gate, exp_w_up, exp_w_down)
    y_sample = encoder_trunk(x_sample, w_in, b_gate, q_norm_g, k_norm_g, rel_bias, w_branch_a, w_branch_b,
                             w_out, ln1_g, ln1_b, ln2_g, ln2_b, ffn_w_gate, ffn_w_up, ffn_w_down,
                             router_w, exp_w_gate, exp_w_up, exp_w_down)
    return (y_prompt, y_sample)
```

```python
import functools
import math

import numpy as np
import jax
import jax.numpy as jnp
from jax import lax
from jax.experimental import pallas as pl
from jax.experimental.pallas import tpu as pltpu

D_MODEL = 2048
DEPTH = 2
HEAD_DIM = 128
A_HEADS = 8
A_WIDTH = A_HEADS * HEAD_DIM
DILATED_PATTERNS = ((128, 1), (512, 4), (2048, 16))
A_HALF = 64
B_HEADS = 8
B_KV_HEADS = 2
B_GROUPS = B_HEADS // B_KV_HEADS
B_WIDTH = B_HEADS * HEAD_DIM
B_KV_WIDTH = B_KV_HEADS * HEAD_DIM
GRID_W = 64
ROPE_THETA = 10000.0
N_BUCKETS = 32
MAX_DISTANCE = 1024
IN_COLS = 3 * A_WIDTH + B_WIDTH + 2 * B_KV_WIDTH + 2 * D_MODEL
D_FF = 5632
N_EXPERTS = 8
TOP_K = 2
D_FF_EXPERT = 7168
DEEPNORM_ALPHA = (2.0 * DEPTH) ** 0.25
LN_EPS = 1e-5
RMS_EPS = 1e-6
NEG_INF = -1e30
ATTN_SCALE = HEAD_DIM ** -0.5

V7X_LANES = 128
V7X_VMEM_BYTES = 64 * 1024 * 1024

PROJ_TM = 512
PROJ_TN = 512
A_BLOCK = 128
B_TQ = 512
B_TK = 2048
BRANCH_TM = 256
FFN_TM = 512
FFN_TF = 512
MOE_TM = 512
MOE_TF = 512
COMBINE_TM = 256

BF16 = jnp.bfloat16
F32 = jnp.float32

assert all(w // (2 * d) == A_HALF for w, d in DILATED_PATTERNS)
assert IN_COLS % PROJ_TN == 0 and D_FF % FFN_TF == 0 and D_FF_EXPERT % MOE_TF == 0


def _vmem_limit(nbytes):
    return int(min(nbytes, V7X_VMEM_BYTES - 4 * 1024 * 1024))


def _params(n_axes, vmem_bytes):
    return pltpu.CompilerParams(
        dimension_semantics=("arbitrary",) * n_axes, vmem_limit_bytes=_vmem_limit(vmem_bytes))


def _any_eq(b, values):
    return functools.reduce(jnp.logical_or, [b == v for v in values])


def _layer_norm_rows(z, g, b):
    mu = jnp.mean(z, axis=-1, keepdims=True)
    zc = z - mu
    var = jnp.mean(zc * zc, axis=-1, keepdims=True)
    return zc * lax.rsqrt(var + LN_EPS) * g + b


_N_QA, _N_KA, _N_VA = 0, A_WIDTH // PROJ_TN, 2 * A_WIDTH // PROJ_TN
_N_QB = 3 * A_WIDTH // PROJ_TN
_N_KVB = _N_QB + B_WIDTH // PROJ_TN
_N_GATE = _N_KVB + 1
_N_TILES = IN_COLS // PROJ_TN
assert 2 * B_KV_WIDTH == PROJ_TN and A_WIDTH % PROJ_TN == 0


def _rms_rope_store(acc, n_heads, g, cos, sin_signed, out_ref):
    lane = lax.broadcasted_iota(jnp.int32, (acc.shape[0], HEAD_DIM), 1)
    low = (lane % (HEAD_DIM // 2)) < (HEAD_DIM // 4)
    for h in range(n_heads):
        xh = acc[:, h * HEAD_DIM:(h + 1) * HEAD_DIM]
        y = xh * lax.rsqrt(jnp.mean(xh * xh, axis=-1, keepdims=True) + RMS_EPS) * g
        rot = jnp.where(low, pltpu.roll(y, HEAD_DIM - HEAD_DIM // 4, 1), pltpu.roll(y, HEAD_DIM // 4, 1))
        out_ref[:, h * HEAD_DIM:(h + 1) * HEAD_DIM] = (y * cos + rot * sin_signed).astype(out_ref.dtype)


def _in_proj_kernel(x_ref, w_ref, bg_ref, qg_ref, kg_ref, cos_ref, sin_ref,
                    qa1, ka1, va1, qa4, ka4, va4, qa16, ka16, va16, qb_ref, kb_ref, vb_ref, gate_ref,
                    acc_ref):
    n = pl.program_id(1)
    tm = x_ref.shape[0]
    acc = jnp.dot(x_ref[...], w_ref[...], preferred_element_type=F32)

    def store_dilated(o1, o4, o16):
        o1[0] = acc.astype(o1.dtype)
        for c in range(acc_ref.shape[0]):
            cs = slice(c * V7X_LANES, (c + 1) * V7X_LANES)
            acc_ref[c] = acc[:, cs]
            for d, o in ((4, o4), (16, o16)):
                for r in range(d):
                    o[r, :, cs] = acc_ref[c, pl.ds(r, tm // d, stride=d), :].astype(o.dtype)

    @pl.when(n < _N_KA)
    def _():
        store_dilated(qa1, qa4, qa16)

    @pl.when(jnp.logical_and(n >= _N_KA, n < _N_VA))
    def _():
        store_dilated(ka1, ka4, ka16)

    @pl.when(jnp.logical_and(n >= _N_VA, n < _N_QB))
    def _():
        store_dilated(va1, va4, va16)

    @pl.when(jnp.logical_and(n >= _N_QB, n < _N_KVB))
    def _():
        _rms_rope_store(acc, PROJ_TN // HEAD_DIM, qg_ref[...], cos_ref[...], sin_ref[...], qb_ref)

    @pl.when(n == _N_KVB)
    def _():
        _rms_rope_store(acc[:, :B_KV_WIDTH], B_KV_HEADS, kg_ref[...], cos_ref[...], sin_ref[...], kb_ref)
        vb_ref[...] = acc[:, B_KV_WIDTH:].astype(vb_ref.dtype)

    @pl.when(n >= _N_GATE)
    def _():
        gate_ref[...] = jax.nn.sigmoid(acc + bg_ref[...]).astype(gate_ref.dtype)


def _in_proj(xb, w, bg, qg, kg, cos, sin_signed):
    T = xb.shape[0]
    tm, tn = PROJ_TM, PROJ_TN
    grid = (T // tm, _N_TILES)

    def clampn(lo, cnt):
        return lambda m, n: (m, jnp.clip(n - lo, 0, cnt - 1))

    def dil_spec(d, lo):
        return pl.BlockSpec((d, tm // d, tn), lambda m, n: (0, m, jnp.clip(n - lo, 0, A_WIDTH // tn - 1)))

    def dil_shape(d):
        return jax.ShapeDtypeStruct((d, T // d, A_WIDTH), BF16)

    in_specs = [
        pl.BlockSpec((tm, D_MODEL), lambda m, n: (m, 0)),
        pl.BlockSpec((D_MODEL, tn), lambda m, n: (0, n)),
        pl.BlockSpec((1, tn), lambda m, n: (0, jnp.clip(n - _N_GATE, 0, 2 * D_MODEL // tn - 1))),
        pl.BlockSpec((1, HEAD_DIM), lambda m, n: (0, 0)),
        pl.BlockSpec((1, HEAD_DIM), lambda m, n: (0, 0)),
        pl.BlockSpec((tm, HEAD_DIM), lambda m, n: (m, 0)),
        pl.BlockSpec((tm, HEAD_DIM), lambda m, n: (m, 0)),
    ]
    out_specs = [dil_spec(1, _N_QA), dil_spec(1, _N_KA), dil_spec(1, _N_VA),
                 dil_spec(4, _N_QA), dil_spec(4, _N_KA), dil_spec(4, _N_VA),
                 dil_spec(16, _N_QA), dil_spec(16, _N_KA), dil_spec(16, _N_VA),
                 pl.BlockSpec((tm, tn), clampn(_N_QB, B_WIDTH // tn)),
                 pl.BlockSpec((tm, B_KV_WIDTH), lambda m, n: (m, 0)),
                 pl.BlockSpec((tm, B_KV_WIDTH), lambda m, n: (m, 0)),
                 pl.BlockSpec((tm, tn), clampn(_N_GATE, 2 * D_MODEL // tn))]
    out_shape = [dil_shape(1)] * 3 + [dil_shape(4)] * 3 + [dil_shape(16)] * 3 + [
        jax.ShapeDtypeStruct((T, B_WIDTH), BF16),
        jax.ShapeDtypeStruct((T, B_KV_WIDTH), BF16),
        jax.ShapeDtypeStruct((T, B_KV_WIDTH), BF16),
        jax.ShapeDtypeStruct((T, 2 * D_MODEL), BF16)]
    vmem = (2 * tm * D_MODEL * 2 + 2 * D_MODEL * tn * 2 + 13 * 2 * tm * tn * 2 + 4 * tm * HEAD_DIM * 4
            + 6 * tm * tn * 4 + (8 << 20))
    return pl.pallas_call(
        _in_proj_kernel, grid=grid, in_specs=in_specs, out_specs=out_specs, out_shape=out_shape,
        scratch_shapes=[pltpu.VMEM((tn // V7X_LANES, tm, V7X_LANES), F32)],
        compiler_params=_params(2, vmem), name="in_proj",
    )(xb, w, bg, qg, kg, cos, sin_signed)


def _attn_a_kernel(q_ref, kp_ref, kc_ref, kn_ref, vp_ref, vc_ref, vn_ref, bias_ref, o_ref, lse_ref,
                   *, first_blocks, last_blocks):
    b = pl.program_id(1)
    win = A_BLOCK + 2 * A_HALF
    lo = jnp.where(_any_eq(b, first_blocks), A_HALF, 0)
    hi = jnp.where(_any_eq(b, last_blocks), A_BLOCK + A_HALF, win)
    col = lax.broadcasted_iota(jnp.int32, (A_BLOCK, win), 1)
    outside = jnp.logical_or(col < lo, col >= hi)
    lane = lax.broadcasted_iota(jnp.int32, (A_BLOCK, V7X_LANES), 1)
    lse_tile = jnp.zeros((A_BLOCK, V7X_LANES), F32)
    for h in range(A_HEADS):
        hs = slice(h * HEAD_DIM, (h + 1) * HEAD_DIM)
        q = q_ref[:, hs]
        k = jnp.concatenate([kp_ref[A_BLOCK - A_HALF:, hs], kc_ref[:, hs], kn_ref[:A_HALF, hs]], axis=0)
        v = jnp.concatenate([vp_ref[A_BLOCK - A_HALF:, hs], vc_ref[:, hs], vn_ref[:A_HALF, hs]], axis=0)
        s = lax.dot_general(q, k, (((1,), (1,)), ((), ())), preferred_element_type=F32)
        s = jnp.where(outside, NEG_INF, s * ATTN_SCALE + bias_ref[h])
        m = jnp.max(s, axis=1, keepdims=True)
        p = jnp.exp(s - m)
        l = jnp.sum(p, axis=1, keepdims=True)
        o = jnp.dot(p.astype(v.dtype), v, preferred_element_type=F32)
        o_ref[:, hs] = o / l
        lse_tile = jnp.where(lane == h, m + jnp.log(l), lse_tile)
    lse_ref[...] = lse_tile


def _attn_a(q, k, v, bias, d, seqs):
    _, L, _ = q.shape
    nb = L // A_BLOCK
    first_blocks = tuple(s0 // d // A_BLOCK for s0, _ in seqs)
    last_blocks = tuple((s0 + sl) // d // A_BLOCK - 1 for s0, sl in seqs)

    def cur(r, b):
        return (r, b, 0)

    def prev(r, b):
        return (r, jnp.where(_any_eq(b, first_blocks), b, b - 1), 0)

    def nxt(r, b):
        return (r, jnp.where(_any_eq(b, last_blocks), b, b + 1), 0)

    blk = (None, A_BLOCK, A_WIDTH)
    win = A_BLOCK + 2 * A_HALF
    in_specs = [pl.BlockSpec(blk, cur),
                pl.BlockSpec(blk, prev), pl.BlockSpec(blk, cur), pl.BlockSpec(blk, nxt),
                pl.BlockSpec(blk, prev), pl.BlockSpec(blk, cur), pl.BlockSpec(blk, nxt),
                pl.BlockSpec((A_HEADS, A_BLOCK, win), lambda r, b: (0, 0, 0))]
    out_specs = [pl.BlockSpec(blk, cur), pl.BlockSpec((None, A_BLOCK, V7X_LANES), cur)]
    out_shape = [jax.ShapeDtypeStruct((d, L, A_WIDTH), F32), jax.ShapeDtypeStruct((d, L, V7X_LANES), F32)]
    vmem = 7 * 2 * A_BLOCK * A_WIDTH * 2 + 2 * A_HEADS * A_BLOCK * win * 4 + 2 * A_BLOCK * A_WIDTH * 4 + (8 << 20)
    return pl.pallas_call(
        functools.partial(_attn_a_kernel, first_blocks=first_blocks, last_blocks=last_blocks),
        grid=(d, nb), in_specs=in_specs, out_specs=out_specs, out_shape=out_shape,
        compiler_params=_params(2, vmem), name=f"attn_a_d{d}",
    )(q, k, k, k, v, v, v, bias)


def _attn_b_kernel(qi_ref, ki_ref, hi_ref, fl_ref, q_ref, k_ref, v_ref, o_ref, m_sc, l_sc, acc_sc):
    step = pl.program_id(0)
    flags = fl_ref[step]

    @pl.when((flags & 1) != 0)
    def _():
        m_sc[...] = jnp.full(m_sc.shape, NEG_INF, F32)
        l_sc[...] = jnp.zeros(l_sc.shape, F32)
        acc_sc[...] = jnp.zeros(acc_sc.shape, F32)

    k = k_ref[...]
    v = v_ref[...]
    for g in range(B_GROUPS):
        q = q_ref[:, g * HEAD_DIM:(g + 1) * HEAD_DIM]
        s = lax.dot_general(q, k, (((1,), (1,)), ((), ())), preferred_element_type=F32) * ATTN_SCALE
        m_prev = m_sc[g]
        m_new = jnp.maximum(m_prev, jnp.max(s, axis=1, keepdims=True))
        alpha = jnp.exp(m_prev - m_new)
        p = jnp.exp(s - m_new[:, :1])
        l_sc[g] = alpha * l_sc[g] + jnp.sum(p, axis=1, keepdims=True)
        acc_sc[g] = alpha * acc_sc[g] + jnp.dot(p.astype(v.dtype), v, preferred_element_type=F32)
        m_sc[g] = m_new

    @pl.when((flags & 2) != 0)
    def _():
        for g in range(B_GROUPS):
            o_ref[:, g * HEAD_DIM:(g + 1) * HEAD_DIM] = (acc_sc[g] / l_sc[g]).astype(o_ref.dtype)


def _attn_b_schedule(seqs, tk):
    qi, ki, hi, fl = [], [], [], []
    for s0, sl in seqs:
        nkv = sl // tk
        for h in range(B_KV_HEADS):
            for qb in range(sl // B_TQ):
                for kb in range(nkv):
                    qi.append(s0 // B_TQ + qb)
                    ki.append(s0 // tk + kb)
                    hi.append(h)
                    fl.append((1 if kb == 0 else 0) | (2 if kb == nkv - 1 else 0))
    return [np.asarray(a, np.int32) for a in (qi, ki, hi, fl)]


def _attn_b(qb, kb, vb, seqs):
    T = qb.shape[0]
    tk = min([B_TK] + [sl for _, sl in seqs])
    assert all(sl % tk == 0 and s0 % tk == 0 and sl % B_TQ == 0 for s0, sl in seqs)
    qi, ki, hi, fl = _attn_b_schedule(seqs, tk)
    gw = B_GROUPS * HEAD_DIM
    grid_spec = pltpu.PrefetchScalarGridSpec(
        num_scalar_prefetch=4, grid=(len(qi),),
        in_specs=[pl.BlockSpec((B_TQ, gw), lambda s, qi, ki, hi, fl: (qi[s], hi[s])),
                  pl.BlockSpec((tk, HEAD_DIM), lambda s, qi, ki, hi, fl: (ki[s], hi[s])),
                  pl.BlockSpec((tk, HEAD_DIM), lambda s, qi, ki, hi, fl: (ki[s], hi[s]))],
        out_specs=pl.BlockSpec((B_TQ, gw), lambda s, qi, ki, hi, fl: (qi[s], hi[s])),
        scratch_shapes=[pltpu.VMEM((B_GROUPS, B_TQ, V7X_LANES), F32)] * 3)
    vmem = 4 * B_TQ * gw * 2 + 4 * tk * HEAD_DIM * 2 + 3 * B_GROUPS * B_TQ * V7X_LANES * 4 + 5 * B_TQ * tk * 4 + (8 << 20)
    return pl.pallas_call(
        _attn_b_kernel, grid_spec=grid_spec, out_shape=jax.ShapeDtypeStruct((T, B_WIDTH), BF16),
        compiler_params=_params(1, vmem), name="attn_b",
    )(jnp.asarray(qi), jnp.asarray(ki), jnp.asarray(hi), jnp.asarray(fl), qb, kb, vb)


def _branch_kernel(*refs, with_router):
    (o1_ref, o4_ref, o16_ref, l1_ref, l4_ref, l16_ref, ob_ref, sga_ref, sgb_ref, x_ref,
     wba_ref, wbb_ref, wo_ref, g_ref, b_ref) = refs[:15]
    rest = refs[15:]
    if with_router:
        rw_ref, x1_ref, x1b_ref, route_ref, o4_sc, o16_sc, l4_sc, l16_sc = rest
    else:
        x1_ref, x1b_ref, o4_sc, o16_sc, l4_sc, l16_sc = rest
    tm = x_ref.shape[0]
    for d, src, dst, lsrc, ldst in ((4, o4_ref, o4_sc, l4_ref, l4_sc), (16, o16_ref, o16_sc, l16_ref, l16_sc)):
        for r in range(d):
            ldst[pl.ds(r, tm // d, stride=d), :] = lsrc[r]
            for h in range(A_HEADS):
                dst[h, pl.ds(r, tm // d, stride=d), :] = src[r, :, h * HEAD_DIM:(h + 1) * HEAD_DIM]
    l1, l4, l16 = l1_ref[0], l4_sc[...], l16_sc[...]
    mx = jnp.maximum(jnp.maximum(l1, l4), l16)
    e1, e4, e16 = jnp.exp(l1 - mx), jnp.exp(l4 - mx), jnp.exp(l16 - mx)
    den = e1 + e4 + e16
    w1, w4, w16 = e1 / den, e4 / den, e16 / den
    parts = []
    for h in range(A_HEADS):
        hs = slice(h * HEAD_DIM, (h + 1) * HEAD_DIM)
        oa_h = w1[:, h:h + 1] * o1_ref[0, :, hs] + w4[:, h:h + 1] * o4_sc[h] + w16[:, h:h + 1] * o16_sc[h]
        parts.append(oa_h.astype(BF16))
    oa = jnp.concatenate(parts, axis=1)
    ya = jnp.dot(oa, wba_ref[...], preferred_element_type=F32)
    yb = jnp.dot(ob_ref[...], wbb_ref[...], preferred_element_type=F32)
    merged = sga_ref[...].astype(F32) * ya + sgb_ref[...].astype(F32) * yb
    y = jnp.dot(merged.astype(BF16), wo_ref[...], preferred_element_type=F32)
    out = _layer_norm_rows(DEEPNORM_ALPHA * x_ref[...] + y, g_ref[...], b_ref[...])
    x1_ref[...] = out
    x1b_ref[...] = out.astype(BF16)
    if with_router:
        logits = jnp.dot(out, rw_ref[...], preferred_element_type=F32, precision=lax.Precision.HIGHEST)
        lane = lax.broadcasted_iota(jnp.int32, logits.shape, 1)
        logits = jnp.where(lane < N_EXPERTS, logits, -jnp.inf)
        v1 = jnp.max(logits, axis=1, keepdims=True)
        i1 = jnp.min(jnp.where(logits == v1, lane, V7X_LANES), axis=1, keepdims=True)
        rem = jnp.where(lane == i1, -jnp.inf, logits)
        v2 = jnp.max(rem, axis=1, keepdims=True)
        i2 = jnp.min(jnp.where(rem == v2, lane, V7X_LANES), axis=1, keepdims=True)
        e2 = jnp.exp(v2 - v1)
        g1 = 1.0 / (1.0 + e2)
        g2 = e2 / (1.0 + e2)
        route = jnp.where(lane == 0, i1.astype(F32),
                          jnp.where(lane == 1, i2.astype(F32),
                                    jnp.where(lane == 2, g1, jnp.where(lane == 3, g2, 0.0))))
        route_ref[...] = route


def _branch(o1, o4, o16, l1, l4, l16, ob, gates, x, wba, wbb, wo, g, b, router_w=None):
    T = x.shape[0]
    tm = BRANCH_TM
    with_router = router_w is not None
    row = lambda m: (m, 0)
    const = lambda m: (0, 0)
    single = pl.Buffered(1)
    in_specs = [
        pl.BlockSpec((1, tm, A_WIDTH), lambda m: (0, m, 0)),
        pl.BlockSpec((4, tm // 4, A_WIDTH), lambda m: (0, m, 0)),
        pl.BlockSpec((16, tm // 16, A_WIDTH), lambda m: (0, m, 0)),
        pl.BlockSpec((1, tm, V7X_LANES), lambda m: (0, m, 0)),
        pl.BlockSpec((4, tm // 4, V7X_LANES), lambda m: (0, m, 0)),
        pl.BlockSpec((16, tm // 16, V7X_LANES), lambda m: (0, m, 0)),
        pl.BlockSpec((tm, B_WIDTH), row),
        pl.BlockSpec((tm, D_MODEL), lambda m: (m, 0)),
        pl.BlockSpec((tm, D_MODEL), lambda m: (m, 1)),
        pl.BlockSpec((tm, D_MODEL), row),
        pl.BlockSpec((A_WIDTH, D_MODEL), const, pipeline_mode=single),
        pl.BlockSpec((B_WIDTH, D_MODEL), const, pipeline_mode=single),
        pl.BlockSpec((D_MODEL, D_MODEL), const, pipeline_mode=single),
        pl.BlockSpec((1, D_MODEL), const),
        pl.BlockSpec((1, D_MODEL), const),
    ]
    args = [o1, o4, o16, l1, l4, l16, ob, gates, gates, x, wba, wbb, wo, g, b]
    out_specs = [pl.BlockSpec((tm, D_MODEL), row), pl.BlockSpec((tm, D_MODEL), row)]
    out_shape = [jax.ShapeDtypeStruct((T, D_MODEL), F32), jax.ShapeDtypeStruct((T, D_MODEL), BF16)]
    if with_router:
        in_specs.append(pl.BlockSpec((D_MODEL, V7X_LANES), const, pipeline_mode=single))
        args.append(router_w)
        out_specs.append(pl.BlockSpec((tm, V7X_LANES), row))
        out_shape.append(jax.ShapeDtypeStruct((T, V7X_LANES), F32))
    scratch = [pltpu.VMEM((A_HEADS, tm, HEAD_DIM), F32), pltpu.VMEM((A_HEADS, tm, HEAD_DIM), F32),
               pltpu.VMEM((tm, V7X_LANES), F32), pltpu.VMEM((tm, V7X_LANES), F32)]
    vmem = ((A_WIDTH + B_WIDTH + D_MODEL) * D_MODEL * 2 + D_MODEL * V7X_LANES * 4
            + 2 * 3 * tm * A_WIDTH * 4 + 2 * tm * A_WIDTH * 4 + 2 * tm * B_WIDTH * 2 + 4 * tm * D_MODEL * 2
            + 2 * tm * D_MODEL * 4 + 2 * tm * D_MODEL * 6 + 8 * tm * D_MODEL * 4 + (8 << 20))
    return pl.pallas_call(
        functools.partial(_branch_kernel, with_router=with_router),
        grid=(T // tm,), in_specs=in_specs, out_specs=out_specs, out_shape=out_shape,
        scratch_shapes=scratch, compiler_params=_params(1, vmem),
        name="branch_router" if with_router else "branch",
    )(*args)


def _swiglu_accumulate(xb, wg_ref, wu_ref, wd_ref, acc_ref):
    g = jnp.dot(xb, wg_ref[...], preferred_element_type=F32)
    u = jnp.dot(xb, wu_ref[...], preferred_element_type=F32)
    a = (g * jax.nn.sigmoid(g) * u).astype(BF16)
    acc_ref[...] += jnp.dot(a, wd_ref[...], preferred_element_type=F32)


def _ffn_dense_kernel(xb_ref, x_ref, wg_ref, wu_ref, wd_ref, g_ref, b_ref, o_ref, ob_ref):
    j = pl.program_id(1)

    @pl.when(j == 0)
    def _():
        o_ref[...] = jnp.zeros(o_ref.shape, F32)

    _swiglu_accumulate(xb_ref[...], wg_ref, wu_ref, wd_ref, o_ref)

    @pl.when(j == pl.num_programs(1) - 1)
    def _():
        out = _layer_norm_rows(DEEPNORM_ALPHA * x_ref[...] + o_ref[...], g_ref[...], b_ref[...])
        o_ref[...] = out
        ob_ref[...] = out.astype(BF16)


def _ffn_dense(xb, x, wg, wu, wd, g, b):
    T = x.shape[0]
    tm, tf = FFN_TM, FFN_TF
    row = lambda m, j: (m, 0)
    const = lambda m, j: (0, 0)
    in_specs = [pl.BlockSpec((tm, D_MODEL), row), pl.BlockSpec((tm, D_MODEL), row),
                pl.BlockSpec((D_MODEL, tf), lambda m, j: (0, j)),
                pl.BlockSpec((D_MODEL, tf), lambda m, j: (0, j)),
                pl.BlockSpec((tf, D_MODEL), lambda m, j: (j, 0)),
                pl.BlockSpec((1, D_MODEL), const), pl.BlockSpec((1, D_MODEL), const)]
    out_specs = [pl.BlockSpec((tm, D_MODEL), row), pl.BlockSpec((tm, D_MODEL), row)]
    out_shape = [jax.ShapeDtypeStruct((T, D_MODEL), F32), jax.ShapeDtypeStruct((T, D_MODEL), BF16)]
    vmem = (2 * tm * D_MODEL * 2 + 2 * tm * D_MODEL * 4 + 2 * 3 * D_MODEL * tf * 2 + 2 * tm * D_MODEL * 6
            + 4 * tm * tf * 4 + 2 * tm * D_MODEL * 4 + (8 << 20))
    return pl.pallas_call(
        _ffn_dense_kernel, grid=(T // tm, D_FF // tf), in_specs=in_specs, out_specs=out_specs,
        out_shape=out_shape, compiler_params=_params(2, vmem), name="ffn_dense",
    )(xb, x, wg, wu, wd, g, b)


def _moe_gather_kernel(src_ref, x_hbm, o_ref, sem):
    n = o_ref.shape[0]

    def row_copy(j, t):
        return pltpu.make_async_copy(x_hbm.at[pl.ds(t, 1), :], o_ref.at[pl.ds(j, 1), :], sem)

    def issue(j, c):
        row_copy(j, src_ref[0, 0, j]).start()
        return c

    def wait(j, c):
        row_copy(j, 0).wait()
        return c

    lax.fori_loop(0, n, issue, 0)
    lax.fori_loop(0, n, wait, 0)


def _moe_gather(x, src):
    n_tiles, _, tm = src.shape
    vmem = 2 * tm * D_MODEL * 4 + (8 << 20)
    return pl.pallas_call(
        _moe_gather_kernel, grid=(n_tiles,),
        in_specs=[pl.BlockSpec((1, 1, tm), lambda i: (i, 0, 0), memory_space=pltpu.SMEM),
                  pl.BlockSpec(memory_space=pl.ANY)],
        out_specs=pl.BlockSpec((tm, D_MODEL), lambda i: (i, 0)),
        out_shape=jax.ShapeDtypeStruct((n_tiles * tm, D_MODEL), F32),
        scratch_shapes=[pltpu.SemaphoreType.DMA(())],
        compiler_params=_params(1, vmem), name="moe_gather",
    )(src, x)


def _moe_ffn_kernel(te_ref, tv_ref, x_ref, wg_ref, wu_ref, wd_ref, o_ref, xb_sc):
    i = pl.program_id(0)
    j = pl.program_id(1)
    valid = tv_ref[i] != 0

    @pl.when(j == 0)
    def _():
        o_ref[...] = jnp.zeros(o_ref.shape, F32)
        xb_sc[...] = x_ref[...].astype(BF16)

    @pl.when(valid)
    def _():
        _swiglu_accumulate(xb_sc[...], wg_ref, wu_ref, wd_ref, o_ref)


def _moe_ffn(xs, tile_expert, tile_valid, wg, wu, wd):
    P = xs.shape[0]
    tm, tf = MOE_TM, MOE_TF
    nj = D_FF_EXPERT // tf

    def jeff(i, j, tv):
        return jnp.where(tv[i] != 0, j, nj - 1)

    grid_spec = pltpu.PrefetchScalarGridSpec(
        num_scalar_prefetch=2, grid=(P // tm, nj),
        in_specs=[pl.BlockSpec((tm, D_MODEL), lambda i, j, te, tv: (i, 0)),
                  pl.BlockSpec((None, D_MODEL, tf), lambda i, j, te, tv: (te[i], 0, jeff(i, j, tv))),
                  pl.BlockSpec((None, D_MODEL, tf), lambda i, j, te, tv: (te[i], 0, jeff(i, j, tv))),
                  pl.BlockSpec((None, tf, D_MODEL), lambda i, j, te, tv: (te[i], jeff(i, j, tv), 0))],
        out_specs=pl.BlockSpec((tm, D_MODEL), lambda i, j, te, tv: (i, 0)),
        scratch_shapes=[pltpu.VMEM((tm, D_MODEL), BF16)])
    vmem = (2 * tm * D_MODEL * 4 + tm * D_MODEL * 2 + 2 * 3 * D_MODEL * tf * 2 + 2 * tm * D_MODEL * 4
            + 4 * tm * tf * 4 + 2 * tm * D_MODEL * 4 + (8 << 20))
    return pl.pallas_call(
        _moe_ffn_kernel, grid_spec=grid_spec, out_shape=jax.ShapeDtypeStruct((P, D_MODEL), F32),
        compiler_params=_params(2, vmem), name="moe_ffn",
    )(tile_expert, tile_valid, xs, wg, wu, wd)


def _moe_combine_kernel(dest_ref, y_hbm, route_ref, x_ref, g_ref, b_ref, o_ref, ob_ref, y0_sc, y1_sc, sem):
    tm = x_ref.shape[0]

    def row_copy(j, k, p):
        dst = y0_sc if k == 0 else y1_sc
        return pltpu.make_async_copy(y_hbm.at[pl.ds(p, 1), :], dst.at[pl.ds(j, 1), :], sem)

    def issue(j, c):
        row_copy(j, 0, dest_ref[0, 0, 2 * j]).start()
        row_copy(j, 1, dest_ref[0, 0, 2 * j + 1]).start()
        return c

    def wait(j, c):
        row_copy(j, 0, 0).wait()
        row_copy(j, 1, 0).wait()
        return c

    lax.fori_loop(0, tm, issue, 0)
    lax.fori_loop(0, tm, wait, 0)
    route = route_ref[...]
    f = route[:, 2:3] * y0_sc[...] + route[:, 3:4] * y1_sc[...]
    out = _layer_norm_rows(DEEPNORM_ALPHA * x_ref[...] + f, g_ref[...], b_ref[...])
    o_ref[...] = out
    ob_ref[...] = out.astype(BF16)


def _moe_combine(y, dest, route, x, g, b):
    T = x.shape[0]
    tm = COMBINE_TM
    row = lambda m: (m, 0)
    const = lambda m: (0, 0)
    vmem = 2 * tm * D_MODEL * 4 + 2 * tm * D_MODEL * 4 + 2 * tm * D_MODEL * 6 + 6 * tm * D_MODEL * 4 + (8 << 20)
    return pl.pallas_call(
        _moe_combine_kernel, grid=(T // tm,),
        in_specs=[pl.BlockSpec((1, 1, 2 * tm), lambda m: (m, 0, 0), memory_space=pltpu.SMEM),
                  pl.BlockSpec(memory_space=pl.ANY),
                  pl.BlockSpec((tm, V7X_LANES), row),
                  pl.BlockSpec((tm, D_MODEL), row),
                  pl.BlockSpec((1, D_MODEL), const), pl.BlockSpec((1, D_MODEL), const)],
        out_specs=[pl.BlockSpec((tm, D_MODEL), row), pl.BlockSpec((tm, D_MODEL), row)],
        out_shape=[jax.ShapeDtypeStruct((T, D_MODEL), F32), jax.ShapeDtypeStruct((T, D_MODEL), BF16)],
        scratch_shapes=[pltpu.VMEM((tm, D_MODEL), F32), pltpu.VMEM((tm, D_MODEL), F32),
                        pltpu.SemaphoreType.DMA(())],
        compiler_params=_params(1, vmem), name="moe_combine",
    )(dest.reshape(T // tm, 1, 2 * tm), y, route, x, g, b)


def _route_plan(route, tm):
    T = route.shape[0]
    n_assign = T * TOP_K
    n_tiles = n_assign // tm + N_EXPERTS
    flat_e = route[:, :TOP_K].astype(jnp.int32).reshape(n_assign)
    onehot = (flat_e[:, None] == jnp.arange(N_EXPERTS, dtype=jnp.int32)[None, :]).astype(jnp.int32)
    csum = jnp.cumsum(onehot, axis=0)
    rank = jnp.sum(csum * onehot, axis=1) - 1
    counts = csum[-1]
    tiles_per = (counts + tm - 1) // tm
    tile_end = jnp.cumsum(tiles_per)
    row_start = (tile_end - tiles_per) * tm
    dest = jnp.sum(onehot * row_start[None, :], axis=1) + rank
    tile_ids = jnp.arange(n_tiles, dtype=jnp.int32)
    n_valid = tile_end[-1]
    tile_valid = (tile_ids < n_valid).astype(jnp.int32)
    owner = jnp.sum((jnp.minimum(tile_ids, n_valid - 1)[:, None] >= tile_end[None, :]).astype(jnp.int32), axis=1)
    tile_expert = jnp.minimum(owner, N_EXPERTS - 1).astype(jnp.int32)
    src = jnp.zeros((n_tiles * tm,), jnp.int32).at[dest].set(jnp.arange(n_assign, dtype=jnp.int32) // TOP_K)
    return dest.reshape(T, TOP_K), src.reshape(n_tiles, 1, tm), tile_expert, tile_valid


def _t5_bucket(rel):
    nb = N_BUCKETS // 2
    max_exact = nb // 2
    n = jnp.abs(rel)
    large = max_exact + (jnp.log(jnp.maximum(n, 1).astype(F32) / max_exact)
                         / math.log(MAX_DISTANCE / max_exact) * (nb - max_exact)).astype(jnp.int32)
    large = jnp.minimum(large, nb - 1)
    return jnp.where(rel > 0, nb, 0) + jnp.where(n < max_exact, n, large)


def _a_bias_table(rel_bias, d):
    i = jnp.arange(A_BLOCK, dtype=jnp.int32)[:, None]
    j = jnp.arange(A_BLOCK + 2 * A_HALF, dtype=jnp.int32)[None, :]
    rel = j - A_HALF - i
    bias = rel_bias[_t5_bucket(rel * d)].astype(F32)
    return jnp.where((jnp.abs(rel) <= A_HALF)[:, :, None], bias, NEG_INF).transpose(2, 0, 1)


def _rope_tables(seqs):
    pos = jnp.concatenate([jnp.arange(sl, dtype=jnp.int32) for _, sl in seqs])
    row = (pos // GRID_W).astype(F32)
    col = (pos % GRID_W).astype(F32)
    n_freq = HEAD_DIM // 4
    inv_freq = ROPE_THETA ** (-jnp.arange(n_freq, dtype=F32) / n_freq)
    ang_r = row[:, None] * inv_freq[None, :]
    ang_c = col[:, None] * inv_freq[None, :]
    ang = jnp.concatenate([ang_r, ang_r, ang_c, ang_c], axis=-1)
    sign = jnp.where((jnp.arange(HEAD_DIM) % (HEAD_DIM // 2)) < (HEAD_DIM // 4), -1.0, 1.0).astype(F32)
    return jnp.cos(ang), jnp.sin(ang) * sign[None, :]


def kernel(x_prompt, x_sample, w_in, b_gate, q_norm_g, k_norm_g, rel_bias, w_branch_a, w_branch_b, w_out,
           ln1_g, ln1_b, ln2_g, ln2_b, ffn_w_gate, ffn_w_up, ffn_w_down, router_w, exp_w_gate, exp_w_up,
           exp_w_down):
    seqs = []
    for xs in (x_prompt, x_sample):
        for _ in range(xs.shape[0]):
            seqs.append((sum(sl for _, sl in seqs), xs.shape[1]))
    seqs = tuple(seqs)
    T = sum(sl for _, sl in seqs)
    assert all(sl % (16 * A_BLOCK) == 0 and s0 % max(PROJ_TM, B_TQ) == 0 for s0, sl in seqs)

    x = jnp.concatenate([x_prompt.reshape(-1, D_MODEL), x_sample.reshape(-1, D_MODEL)], axis=0)
    xb = x.astype(BF16)
    cos, sin_signed = _rope_tables(seqs)
    biases = [_a_bias_table(rel_bias, d) for _, d in DILATED_PATTERNS]
    router_pad = jnp.pad(router_w.astype(F32), ((0, 0), (0, 0), (0, V7X_LANES - N_EXPERTS)))

    for l in range(DEPTH):
        (qa1, ka1, va1, qa4, ka4, va4, qa16, ka16, va16, qb, kb, vb, gates) = _in_proj(
            xb, w_in[l].astype(BF16), b_gate[l].reshape(1, 2 * D_MODEL),
            q_norm_g[l].reshape(1, HEAD_DIM), k_norm_g[l].reshape(1, HEAD_DIM), cos, sin_signed)
        o1, l1 = _attn_a(qa1, ka1, va1, biases[0], 1, seqs)
        o4, l4 = _attn_a(qa4, ka4, va4, biases[1], 4, seqs)
        o16, l16 = _attn_a(qa16, ka16, va16, biases[2], 16, seqs)
        ob = _attn_b(qb, kb, vb, seqs)
        moe = l % 2 == 1
        res = _branch(o1, o4, o16, l1, l4, l16, ob, gates, x,
                      w_branch_a[l].astype(BF16), w_branch_b[l].astype(BF16), w_out[l].astype(BF16),
                      ln1_g[l].reshape(1, D_MODEL), ln1_b[l].reshape(1, D_MODEL),
                      router_pad[l // 2] if moe else None)
        g2, b2 = ln2_g[l].reshape(1, D_MODEL), ln2_b[l].reshape(1, D_MODEL)
        j = l // 2
        if not moe:
            x1, x1b = res
            x, xb = _ffn_dense(x1b, x1, ffn_w_gate[j].astype(BF16), ffn_w_up[j].astype(BF16),
                               ffn_w_down[j].astype(BF16), g2, b2)
        else:
            x1, x1b, route = res
            dest, src, tile_expert, tile_valid = _route_plan(route, MOE_TM)
            xs = _moe_gather(x1, src)
            ys = _moe_ffn(xs, tile_expert, tile_valid, exp_w_gate[j].astype(BF16), exp_w_up[j].astype(BF16),
                          exp_w_down[j].astype(BF16))
            x, xb = _moe_combine(ys, dest, route, x1, g2, b2)

    n_prompt = x_prompt.shape[0] * x_prompt.shape[1]
    return (x[:n_prompt].reshape(x_prompt.shape), x[n_prompt:].reshape(x_sample.shape))
```

```python
import functools
import math

import numpy as np
import jax
import jax.numpy as jnp
from jax import lax
from jax.experimental import pallas as pl
from jax.experimental.pallas import tpu as pltpu

D_MODEL = 2048
DEPTH = 2
HEAD_DIM = 128
A_HEADS = 8
A_WIDTH = A_HEADS * HEAD_DIM
DILATED_PATTERNS = ((128, 1), (512, 4), (2048, 16))
A_HALF = 64
B_HEADS = 8
B_KV_HEADS = 2
B_GROUPS = B_HEADS // B_KV_HEADS
B_WIDTH = B_HEADS * HEAD_DIM
B_KV_WIDTH = B_KV_HEADS * HEAD_DIM
GRID_W = 64
ROPE_THETA = 10000.0
N_BUCKETS = 32
MAX_DISTANCE = 1024
IN_COLS = 3 * A_WIDTH + B_WIDTH + 2 * B_KV_WIDTH + 2 * D_MODEL
D_FF = 5632
N_EXPERTS = 8
TOP_K = 2
D_FF_EXPERT = 7168
DEEPNORM_ALPHA = (2.0 * DEPTH) ** 0.25
LN_EPS = 1e-5
RMS_EPS = 1e-6
NEG_INF = -1e30
ATTN_SCALE = HEAD_DIM ** -0.5

V7X_LANES = 128
V7X_VMEM_BYTES = 64 * 1024 * 1024

PROJ_TM = 1024
PROJ_TN = 512
A_BLOCK = 256
A_SUB = 128
B_TQ = 1024
B_TK = 4096
B_CK = 512
BRANCH_TM = 256
FFN_TM = 512
FFN_TF = 512
MOE_TM = 512
MOE_TF = 512
COMBINE_TM = 256

BF16 = jnp.bfloat16
F32 = jnp.float32

assert all(w // (2 * d) == A_HALF for w, d in DILATED_PATTERNS)
assert IN_COLS % PROJ_TN == 0 and D_FF % FFN_TF == 0 and D_FF_EXPERT % MOE_TF == 0


def _vmem_limit(nbytes):
    return int(min(nbytes, V7X_VMEM_BYTES - 4 * 1024 * 1024))


def _params(n_axes, vmem_bytes):
    return pltpu.CompilerParams(
        dimension_semantics=("arbitrary",) * n_axes, vmem_limit_bytes=_vmem_limit(vmem_bytes))


def _any_eq(b, values):
    return functools.reduce(jnp.logical_or, [b == v for v in values])


def _layer_norm_rows(z, g, b):
    mu = jnp.mean(z, axis=-1, keepdims=True)
    zc = z - mu
    var = jnp.mean(zc * zc, axis=-1, keepdims=True)
    return zc * lax.rsqrt(var + LN_EPS) * g + b


_N_QA, _N_KA, _N_VA = 0, A_WIDTH // PROJ_TN, 2 * A_WIDTH // PROJ_TN
_N_QB = 3 * A_WIDTH // PROJ_TN
_N_KVB = _N_QB + B_WIDTH // PROJ_TN
_N_GATE = _N_KVB + 1
_N_TILES = IN_COLS // PROJ_TN
assert 2 * B_KV_WIDTH == PROJ_TN and A_WIDTH % PROJ_TN == 0


def _rms_rope_store(acc, n_heads, g, cos, sin_signed, post_scale, out_ref):
    lane = lax.broadcasted_iota(jnp.int32, (acc.shape[0], HEAD_DIM), 1)
    low = (lane % (HEAD_DIM // 2)) < (HEAD_DIM // 4)
    for h in range(n_heads):
        xh = acc[:, h * HEAD_DIM:(h + 1) * HEAD_DIM]
        y = xh * lax.rsqrt(jnp.mean(xh * xh, axis=-1, keepdims=True) + RMS_EPS) * g
        rot = jnp.where(low, pltpu.roll(y, HEAD_DIM - HEAD_DIM // 4, 1), pltpu.roll(y, HEAD_DIM // 4, 1))
        roped = y * cos + rot * sin_signed
        if post_scale != 1.0:
            roped = roped * post_scale
        out_ref[:, h * HEAD_DIM:(h + 1) * HEAD_DIM] = roped.astype(out_ref.dtype)


B_Q_PRESCALE = ATTN_SCALE * math.log2(math.e)


def _in_proj_kernel(x_ref, w_ref, bg_ref, qg_ref, kg_ref, cos_ref, sin_ref,
                    a1_ref, a4_ref, a16_ref, qb_ref, kb_ref, vb_ref, gate_ref, acc_ref):
    n = pl.program_id(1)
    tm = x_ref.shape[0]

    def product():
        return jnp.dot(x_ref[...], w_ref[...], preferred_element_type=F32)

    @pl.when(n < _N_QB)
    def _():
        acc = product()
        a1_ref[0] = acc.astype(a1_ref.dtype)
        for c in range(acc_ref.shape[0]):
            cs = slice(c * V7X_LANES, (c + 1) * V7X_LANES)
            acc_ref[c] = acc[:, cs]
            for d, o in ((4, a4_ref), (16, a16_ref)):
                for r in range(d):
                    o[r, :, cs] = acc_ref[c, pl.ds(r, tm // d, stride=d), :].astype(o.dtype)

    @pl.when(jnp.logical_and(n >= _N_QB, n < _N_KVB))
    def _():
        _rms_rope_store(product(), PROJ_TN // HEAD_DIM, qg_ref[...], cos_ref[...], sin_ref[...], B_Q_PRESCALE,
                        qb_ref)

    @pl.when(n == _N_KVB)
    def _():
        acc = product()
        _rms_rope_store(acc[:, :B_KV_WIDTH], B_KV_HEADS, kg_ref[...], cos_ref[...], sin_ref[...], 1.0, kb_ref)
        vb_ref[...] = acc[:, B_KV_WIDTH:].astype(vb_ref.dtype)

    @pl.when(n >= _N_GATE)
    def _():
        gate_ref[...] = jax.nn.sigmoid(product() + bg_ref[...]).astype(gate_ref.dtype)


def _in_proj(xb, w, bg, qg, kg, cos, sin_signed):
    T = xb.shape[0]
    tm, tn = PROJ_TM, PROJ_TN
    grid = (T // tm, _N_TILES)
    a_tiles = A_WIDTH // tn

    def clampn(lo, cnt):
        return lambda m, n: (m, jnp.clip(n - lo, 0, cnt - 1))

    def dil_spec(d):
        def index(m, n):
            nc = jnp.minimum(n, 3 * a_tiles - 1)
            return (nc // a_tiles, 0, m, nc % a_tiles)
        return pl.BlockSpec((None, d, tm // d, tn), index)

    def dil_shape(d):
        return jax.ShapeDtypeStruct((3, d, T // d, A_WIDTH), BF16)

    in_specs = [
        pl.BlockSpec((tm, D_MODEL), lambda m, n: (m, 0)),
        pl.BlockSpec((D_MODEL, tn), lambda m, n: (0, n)),
        pl.BlockSpec((1, tn), lambda m, n: (0, jnp.clip(n - _N_GATE, 0, 2 * D_MODEL // tn - 1))),
        pl.BlockSpec((1, HEAD_DIM), lambda m, n: (0, 0)),
        pl.BlockSpec((1, HEAD_DIM), lambda m, n: (0, 0)),
        pl.BlockSpec((tm, HEAD_DIM), lambda m, n: (m, 0)),
        pl.BlockSpec((tm, HEAD_DIM), lambda m, n: (m, 0)),
    ]
    out_specs = [dil_spec(1), dil_spec(4), dil_spec(16),
                 pl.BlockSpec((tm, tn), clampn(_N_QB, B_WIDTH // tn)),
                 pl.BlockSpec((tm, B_KV_WIDTH), lambda m, n: (m, 0)),
                 pl.BlockSpec((tm, B_KV_WIDTH), lambda m, n: (m, 0)),
                 pl.BlockSpec((tm, tn), clampn(_N_GATE, 2 * D_MODEL // tn))]
    out_shape = [dil_shape(1), dil_shape(4), dil_shape(16),
                 jax.ShapeDtypeStruct((T, B_WIDTH), BF16),
                 jax.ShapeDtypeStruct((T, B_KV_WIDTH), BF16),
                 jax.ShapeDtypeStruct((T, B_KV_WIDTH), BF16),
                 jax.ShapeDtypeStruct((T, 2 * D_MODEL), BF16)]
    vmem = (2 * tm * D_MODEL * 2 + 2 * D_MODEL * tn * 2 + 7 * 2 * tm * tn * 2 + 4 * tm * HEAD_DIM * 4
            + 6 * tm * tn * 4 + (8 << 20))
    return pl.pallas_call(
        _in_proj_kernel, grid=grid, in_specs=in_specs, out_specs=out_specs, out_shape=out_shape,
        scratch_shapes=[pltpu.VMEM((tn // V7X_LANES, tm, V7X_LANES), F32)],
        compiler_params=_params(2, vmem), name="in_proj",
    )(xb, w, bg, qg, kg, cos, sin_signed)


def _attn_a_kernel(q_ref, kp_ref, kc_ref, kn_ref, vp_ref, vc_ref, vn_ref, bias_ref, o_ref, lse_ref,
                   *, first_blocks, last_blocks):
    b = pl.program_id(1)
    win = A_SUB + 2 * A_HALF
    is_first = _any_eq(b, first_blocks)
    is_last = _any_eq(b, last_blocks)
    col = lax.broadcasted_iota(jnp.int32, (A_SUB, win), 1)
    lane = lax.broadcasted_iota(jnp.int32, (A_SUB, V7X_LANES), 1)
    n_sub = A_BLOCK // A_SUB

    def window(prev_ref, cur_ref, next_ref, start, hs):
        parts = []
        if start < 0:
            parts.append(prev_ref[A_BLOCK + start:, hs])
        parts.append(cur_ref[max(start, 0):min(start + win, A_BLOCK), hs])
        if start + win > A_BLOCK:
            parts.append(next_ref[:start + win - A_BLOCK, hs])
        return parts[0] if len(parts) == 1 else jnp.concatenate(parts, axis=0)

    for sb in range(n_sub):
        start = sb * A_SUB - A_HALF
        rows = slice(sb * A_SUB, (sb + 1) * A_SUB)
        lo = jnp.where(is_first, -start, 0) if start < 0 else 0
        hi = jnp.where(is_last, A_BLOCK - start, win) if start + win > A_BLOCK else win
        outside = jnp.logical_or(col < lo, col >= hi)
        lse_tile = jnp.zeros((A_SUB, V7X_LANES), F32)
        for h in range(A_HEADS):
            hs = slice(h * HEAD_DIM, (h + 1) * HEAD_DIM)
            q = q_ref[rows, hs]
            k = window(kp_ref, kc_ref, kn_ref, start, hs)
            v = window(vp_ref, vc_ref, vn_ref, start, hs)
            s = lax.dot_general(q, k, (((1,), (1,)), ((), ())), preferred_element_type=F32)
            s = jnp.where(outside, NEG_INF, s * ATTN_SCALE + bias_ref[h])
            m = jnp.max(s, axis=1, keepdims=True)
            p = jnp.exp(s - m)
            l = jnp.sum(p, axis=1, keepdims=True)
            o = jnp.dot(p.astype(v.dtype), v, preferred_element_type=F32)
            o_ref[rows, hs] = o / l
            lse_tile = jnp.where(lane == h, m + jnp.log(l), lse_tile)
        lse_ref[rows, :] = lse_tile


def _attn_a(a, bias, d, seqs):
    _, _, L, _ = a.shape
    nb = L // A_BLOCK
    assert all(s0 % (d * A_BLOCK) == 0 and sl % (d * A_BLOCK) == 0 for s0, sl in seqs)
    first_blocks = tuple(s0 // d // A_BLOCK for s0, _ in seqs)
    last_blocks = tuple((s0 + sl) // d // A_BLOCK - 1 for s0, sl in seqs)

    def cur(which):
        return lambda r, b: (which, r, b, 0)

    def prev(which):
        return lambda r, b: (which, r, jnp.where(_any_eq(b, first_blocks), b, b - 1), 0)

    def nxt(which):
        return lambda r, b: (which, r, jnp.where(_any_eq(b, last_blocks), b, b + 1), 0)

    blk = (None, None, A_BLOCK, A_WIDTH)
    win = A_SUB + 2 * A_HALF
    in_specs = [pl.BlockSpec(blk, cur(0)),
                pl.BlockSpec(blk, prev(1)), pl.BlockSpec(blk, cur(1)), pl.BlockSpec(blk, nxt(1)),
                pl.BlockSpec(blk, prev(2)), pl.BlockSpec(blk, cur(2)), pl.BlockSpec(blk, nxt(2)),
                pl.BlockSpec((A_HEADS, A_SUB, win), lambda r, b: (0, 0, 0))]
    out_idx = lambda r, b: (r, b, 0)
    out_specs = [pl.BlockSpec((None, A_BLOCK, A_WIDTH), out_idx), pl.BlockSpec((None, A_BLOCK, V7X_LANES), out_idx)]
    out_shape = [jax.ShapeDtypeStruct((d, L, A_WIDTH), F32), jax.ShapeDtypeStruct((d, L, V7X_LANES), F32)]
    vmem = 7 * 2 * A_BLOCK * A_WIDTH * 2 + 2 * A_HEADS * A_SUB * win * 4 + 2 * A_BLOCK * A_WIDTH * 4 + (8 << 20)
    return pl.pallas_call(
        functools.partial(_attn_a_kernel, first_blocks=first_blocks, last_blocks=last_blocks),
        grid=(d, nb), in_specs=in_specs, out_specs=out_specs, out_shape=out_shape,
        compiler_params=_params(2, vmem), name=f"attn_a_d{d}",
    )(a, a, a, a, a, a, a, bias)


def _attn_b_kernel(qi_ref, ki_ref, hi_ref, fl_ref, q_ref, k_ref, v_ref, o_ref, m_sc, l_sc, acc_sc):
    step = pl.program_id(0)
    flags = fl_ref[step]

    @pl.when((flags & 1) != 0)
    def _():
        m_sc[...] = jnp.full(m_sc.shape, NEG_INF, F32)
        l_sc[...] = jnp.zeros(l_sc.shape, F32)
        acc_sc[...] = jnp.zeros(acc_sc.shape, F32)


    tq = q_ref.shape[0]
    q = jnp.concatenate([q_ref[:, g * HEAD_DIM:(g + 1) * HEAD_DIM] for g in range(B_GROUPS)], axis=0)

    n_chunks = k_ref.shape[0] // B_CK

    def chunk_rows(ci):
        return pl.ds(pl.multiple_of(ci * B_CK, B_CK), B_CK)

    def scores(ci):
        return lax.dot_general(q, k_ref[chunk_rows(ci), :], (((1,), (1,)), ((), ())),
                               preferred_element_type=F32)

    def softmax_pv(ci, t):
        v = v_ref[chunk_rows(ci), :]
        m_prev = m_sc[...]
        m_new = jnp.maximum(m_prev, jnp.max(t, axis=1, keepdims=True))
        alpha = jnp.exp2(m_prev - m_new)
        p = jnp.exp2(t - jnp.concatenate([m_new] * (B_CK // V7X_LANES), axis=1))
        l_sc[...] = alpha * l_sc[...] + jnp.sum(p, axis=1, keepdims=True)
        acc_sc[...] = alpha * acc_sc[...] + jnp.dot(p.astype(v.dtype), v, preferred_element_type=F32)
        m_sc[...] = m_new

    def chunk(ci, carry):
        softmax_pv(ci, scores(ci))
        return carry

    lax.fori_loop(0, n_chunks, chunk, 0)

    @pl.when((flags & 2) != 0)
    def _():
        for g in range(B_GROUPS):
            rows = slice(g * tq, (g + 1) * tq)
            o_ref[:, g * HEAD_DIM:(g + 1) * HEAD_DIM] = (acc_sc[rows, :] / l_sc[rows, :]).astype(o_ref.dtype)


def _attn_b_schedule(seqs, tk):
    qi, ki, hi, fl = [], [], [], []
    for s0, sl in seqs:
        nkv = sl // tk
        for h in range(B_KV_HEADS):
            for qb in range(sl // B_TQ):
                for kb in range(nkv):
                    qi.append(s0 // B_TQ + qb)
                    ki.append(s0 // tk + kb)
                    hi.append(h)
                    fl.append((1 if kb == 0 else 0) | (2 if kb == nkv - 1 else 0))
    return [np.asarray(a, np.int32) for a in (qi, ki, hi, fl)]


def _attn_b(qb, kb, vb, seqs):
    T = qb.shape[0]
    tk = min([B_TK] + [sl for _, sl in seqs])
    assert all(sl % tk == 0 and s0 % tk == 0 and sl % B_TQ == 0 for s0, sl in seqs)
    qi, ki, hi, fl = _attn_b_schedule(seqs, tk)
    gw = B_GROUPS * HEAD_DIM
    rows = B_GROUPS * B_TQ
    grid_spec = pltpu.PrefetchScalarGridSpec(
        num_scalar_prefetch=4, grid=(len(qi),),
        in_specs=[pl.BlockSpec((B_TQ, gw), lambda s, qi, ki, hi, fl: (qi[s], hi[s])),
                  pl.BlockSpec((tk, HEAD_DIM), lambda s, qi, ki, hi, fl: (ki[s], hi[s])),
                  pl.BlockSpec((tk, HEAD_DIM), lambda s, qi, ki, hi, fl: (ki[s], hi[s]))],
        out_specs=pl.BlockSpec((B_TQ, gw), lambda s, qi, ki, hi, fl: (qi[s], hi[s])),
        scratch_shapes=[pltpu.VMEM((rows, V7X_LANES), F32)] * 3)
    vmem = 4 * B_TQ * gw * 2 + 4 * tk * HEAD_DIM * 2 + 3 * rows * V7X_LANES * 4 + 4 * rows * B_CK * 4 + (8 << 20)
    return pl.pallas_call(
        _attn_b_kernel, grid_spec=grid_spec, out_shape=jax.ShapeDtypeStruct((T, B_WIDTH), BF16),
        compiler_params=_params(1, vmem), name="attn_b",
    )(jnp.asarray(qi), jnp.asarray(ki), jnp.asarray(hi), jnp.asarray(fl), qb, kb, vb)


def _branch_kernel(*refs, with_router):
    (o1_ref, o4_ref, o16_ref, l1_ref, l4_ref, l16_ref, ob_ref, sga_ref, sgb_ref, x_ref,
     wba_ref, wbb_ref, wo_ref, g_ref, b_ref) = refs[:15]
    rest = refs[15:]
    if with_router:
        rw_ref, x1_ref, x1b_ref, route_ref, o4_sc, o16_sc, l4_sc, l16_sc = rest
    else:
        x1_ref, x1b_ref, o4_sc, o16_sc, l4_sc, l16_sc = rest
    tm = x_ref.shape[0]
    for d, src, dst, lsrc, ldst in ((4, o4_ref, o4_sc, l4_ref, l4_sc), (16, o16_ref, o16_sc, l16_ref, l16_sc)):
        for r in range(d):
            ldst[pl.ds(r, tm // d, stride=d), :] = lsrc[r]
            for h in range(A_HEADS):
                dst[h, pl.ds(r, tm // d, stride=d), :] = src[r, :, h * HEAD_DIM:(h + 1) * HEAD_DIM]
    l1, l4, l16 = l1_ref[0], l4_sc[...], l16_sc[...]
    mx = jnp.maximum(jnp.maximum(l1, l4), l16)
    e1, e4, e16 = jnp.exp(l1 - mx), jnp.exp(l4 - mx), jnp.exp(l16 - mx)
    den = e1 + e4 + e16
    w1, w4, w16 = e1 / den, e4 / den, e16 / den
    parts = []
    for h in range(A_HEADS):
        hs = slice(h * HEAD_DIM, (h + 1) * HEAD_DIM)
        oa_h = w1[:, h:h + 1] * o1_ref[0, :, hs] + w4[:, h:h + 1] * o4_sc[h] + w16[:, h:h + 1] * o16_sc[h]
        parts.append(oa_h.astype(BF16))
    oa = jnp.concatenate(parts, axis=1)
    ya = jnp.dot(oa, wba_ref[...], preferred_element_type=F32)
    yb = jnp.dot(ob_ref[...], wbb_ref[...], preferred_element_type=F32)
    merged = sga_ref[...].astype(F32) * ya + sgb_ref[...].astype(F32) * yb
    y = jnp.dot(merged.astype(BF16), wo_ref[...], preferred_element_type=F32)
    out = _layer_norm_rows(DEEPNORM_ALPHA * x_ref[...] + y, g_ref[...], b_ref[...])
    x1_ref[...] = out
    x1b_ref[...] = out.astype(BF16)
    if with_router:
        out_hi = out.astype(BF16)
        out_lo = (out - out_hi.astype(F32)).astype(BF16)
        both = jnp.dot(out_hi, rw_ref[...], preferred_element_type=F32)
        logits = (both[:, :V7X_LANES] + both[:, V7X_LANES:]
                  + jnp.dot(out_lo, rw_ref[:, :V7X_LANES], preferred_element_type=F32))
        lane = lax.broadcasted_iota(jnp.int32, logits.shape, 1)
        logits = jnp.where(lane < N_EXPERTS, logits, -jnp.inf)
        v1 = jnp.max(logits, axis=1, keepdims=True)
        i1 = jnp.min(jnp.where(logits == v1, lane, V7X_LANES), axis=1, keepdims=True)
        rem = jnp.where(lane == i1, -jnp.inf, logits)
        v2 = jnp.max(rem, axis=1, keepdims=True)
        i2 = jnp.min(jnp.where(rem == v2, lane, V7X_LANES), axis=1, keepdims=True)
        e2 = jnp.exp(v2 - v1)
        g1 = 1.0 / (1.0 + e2)
        g2 = e2 / (1.0 + e2)
        route = jnp.where(lane == 0, i1.astype(F32),
                          jnp.where(lane == 1, i2.astype(F32),
                                    jnp.where(lane == 2, g1, jnp.where(lane == 3, g2, 0.0))))
        route_ref[...] = route


def _branch(o1, o4, o16, l1, l4, l16, ob, gates, x, wba, wbb, wo, g, b, router_w=None):
    T = x.shape[0]
    tm = BRANCH_TM
    with_router = router_w is not None
    row = lambda m: (m, 0)
    const = lambda m: (0, 0)
    single = pl.Buffered(1)
    in_specs = [
        pl.BlockSpec((1, tm, A_WIDTH), lambda m: (0, m, 0)),
        pl.BlockSpec((4, tm // 4, A_WIDTH), lambda m: (0, m, 0)),
        pl.BlockSpec((16, tm // 16, A_WIDTH), lambda m: (0, m, 0)),
        pl.BlockSpec((1, tm, V7X_LANES), lambda m: (0, m, 0)),
        pl.BlockSpec((4, tm // 4, V7X_LANES), lambda m: (0, m, 0)),
        pl.BlockSpec((16, tm // 16, V7X_LANES), lambda m: (0, m, 0)),
        pl.BlockSpec((tm, B_WIDTH), row),
        pl.BlockSpec((tm, D_MODEL), lambda m: (m, 0)),
        pl.BlockSpec((tm, D_MODEL), lambda m: (m, 1)),
        pl.BlockSpec((tm, D_MODEL), row),
        pl.BlockSpec((A_WIDTH, D_MODEL), const, pipeline_mode=single),
        pl.BlockSpec((B_WIDTH, D_MODEL), const, pipeline_mode=single),
        pl.BlockSpec((D_MODEL, D_MODEL), const, pipeline_mode=single),
        pl.BlockSpec((1, D_MODEL), const),
        pl.BlockSpec((1, D_MODEL), const),
    ]
    args = [o1, o4, o16, l1, l4, l16, ob, gates, gates, x, wba, wbb, wo, g, b]
    out_specs = [pl.BlockSpec((tm, D_MODEL), row), pl.BlockSpec((tm, D_MODEL), row)]
    out_shape = [jax.ShapeDtypeStruct((T, D_MODEL), F32), jax.ShapeDtypeStruct((T, D_MODEL), BF16)]
    if with_router:
        in_specs.append(pl.BlockSpec((D_MODEL, 2 * V7X_LANES), const, pipeline_mode=single))
        args.append(router_w)
        out_specs.append(pl.BlockSpec((tm, V7X_LANES), row))
        out_shape.append(jax.ShapeDtypeStruct((T, V7X_LANES), F32))
    scratch = [pltpu.VMEM((A_HEADS, tm, HEAD_DIM), F32), pltpu.VMEM((A_HEADS, tm, HEAD_DIM), F32),
               pltpu.VMEM((tm, V7X_LANES), F32), pltpu.VMEM((tm, V7X_LANES), F32)]
    vmem = ((A_WIDTH + B_WIDTH + D_MODEL) * D_MODEL * 2 + D_MODEL * V7X_LANES * 4
            + 2 * 3 * tm * A_WIDTH * 4 + 2 * tm * A_WIDTH * 4 + 2 * tm * B_WIDTH * 2 + 4 * tm * D_MODEL * 2
            + 2 * tm * D_MODEL * 4 + 2 * tm * D_MODEL * 6 + 8 * tm * D_MODEL * 4 + (8 << 20))
    return pl.pallas_call(
        functools.partial(_branch_kernel, with_router=with_router),
        grid=(T // tm,), in_specs=in_specs, out_specs=out_specs, out_shape=out_shape,
        scratch_shapes=scratch, compiler_params=_params(1, vmem),
        name="branch_router" if with_router else "branch",
    )(*args)


def _swiglu_accumulate(xb, wg_ref, wu_ref, wd_ref, acc_ref):
    g = jnp.dot(xb, wg_ref[...], preferred_element_type=F32)
    u = jnp.dot(xb, wu_ref[...], preferred_element_type=F32)
    a = (g * jax.nn.sigmoid(g) * u).astype(BF16)
    acc_ref[...] += jnp.dot(a, wd_ref[...], preferred_element_type=F32)


def _ffn_dense_kernel(xb_ref, x_ref, wg_ref, wu_ref, wd_ref, g_ref, b_ref, o_ref, ob_ref):
    j = pl.program_id(1)

    @pl.when(j == 0)
    def _():
        o_ref[...] = jnp.zeros(o_ref.shape, F32)

    _swiglu_accumulate(xb_ref[...], wg_ref, wu_ref, wd_ref, o_ref)

    @pl.when(j == pl.num_programs(1) - 1)
    def _():
        out = _layer_norm_rows(DEEPNORM_ALPHA * x_ref[...] + o_ref[...], g_ref[...], b_ref[...])
        o_ref[...] = out
        ob_ref[...] = out.astype(BF16)


def _ffn_dense(xb, x, wg, wu, wd, g, b):
    T = x.shape[0]
    tm, tf = FFN_TM, FFN_TF
    row = lambda m, j: (m, 0)
    const = lambda m, j: (0, 0)
    in_specs = [pl.BlockSpec((tm, D_MODEL), row), pl.BlockSpec((tm, D_MODEL), row),
                pl.BlockSpec((D_MODEL, tf), lambda m, j: (0, j)),
                pl.BlockSpec((D_MODEL, tf), lambda m, j: (0, j)),
                pl.BlockSpec((tf, D_MODEL), lambda m, j: (j, 0)),
                pl.BlockSpec((1, D_MODEL), const), pl.BlockSpec((1, D_MODEL), const)]
    out_specs = [pl.BlockSpec((tm, D_MODEL), row), pl.BlockSpec((tm, D_MODEL), row)]
    out_shape = [jax.ShapeDtypeStruct((T, D_MODEL), F32), jax.ShapeDtypeStruct((T, D_MODEL), BF16)]
    vmem = (2 * tm * D_MODEL * 2 + 2 * tm * D_MODEL * 4 + 2 * 3 * D_MODEL * tf * 2 + 2 * tm * D_MODEL * 6
            + 4 * tm * tf * 4 + 2 * tm * D_MODEL * 4 + (8 << 20))
    return pl.pallas_call(
        _ffn_dense_kernel, grid=(T // tm, D_FF // tf), in_specs=in_specs, out_specs=out_specs,
        out_shape=out_shape, compiler_params=_params(2, vmem), name="ffn_dense",
    )(xb, x, wg, wu, wd, g, b)


def _moe_gather_kernel(src_ref, x_hbm, o_ref, sem):
    n = o_ref.shape[0]

    def row_copy(j, t):
        return pltpu.make_async_copy(x_hbm.at[pl.ds(t, 1), :], o_ref.at[pl.ds(j, 1), :], sem)

    def issue(j, c):
        row_copy(j, src_ref[0, 0, j]).start()
        return c

    def wait(j, c):
        row_copy(j, 0).wait()
        return c

    lax.fori_loop(0, n, issue, 0)
    lax.fori_loop(0, n, wait, 0)


def _moe_gather(x, src):
    n_tiles, _, tm = src.shape
    vmem = 2 * tm * D_MODEL * 4 + (8 << 20)
    return pl.pallas_call(
        _moe_gather_kernel, grid=(n_tiles,),
        in_specs=[pl.BlockSpec((1, 1, tm), lambda i: (i, 0, 0), memory_space=pltpu.SMEM),
                  pl.BlockSpec(memory_space=pl.ANY)],
        out_specs=pl.BlockSpec((tm, D_MODEL), lambda i: (i, 0)),
        out_shape=jax.ShapeDtypeStruct((n_tiles * tm, D_MODEL), F32),
        scratch_shapes=[pltpu.SemaphoreType.DMA(())],
        compiler_params=_params(1, vmem), name="moe_gather",
    )(src, x)


def _moe_ffn_kernel(te_ref, tv_ref, x_ref, wg_ref, wu_ref, wd_ref, o_ref, xb_sc):
    i = pl.program_id(0)
    j = pl.program_id(1)
    valid = tv_ref[i] != 0

    @pl.when(j == 0)
    def _():
        o_ref[...] = jnp.zeros(o_ref.shape, F32)
        xb_sc[...] = x_ref[...].astype(BF16)

    @pl.when(valid)
    def _():
        _swiglu_accumulate(xb_sc[...], wg_ref, wu_ref, wd_ref, o_ref)


def _moe_ffn(xs, tile_expert, tile_valid, wg, wu, wd):
    P = xs.shape[0]
    tm, tf = MOE_TM, MOE_TF
    nj = D_FF_EXPERT // tf

    def jeff(i, j, tv):
        return jnp.where(tv[i] != 0, j, nj - 1)

    grid_spec = pltpu.PrefetchScalarGridSpec(
        num_scalar_prefetch=2, grid=(P // tm, nj),
        in_specs=[pl.BlockSpec((tm, D_MODEL), lambda i, j, te, tv: (i, 0)),
                  pl.BlockSpec((None, D_MODEL, tf), lambda i, j, te, tv: (te[i], 0, jeff(i, j, tv))),
                  pl.BlockSpec((None, D_MODEL, tf), lambda i, j, te, tv: (te[i], 0, jeff(i, j, tv))),
                  pl.BlockSpec((None, tf, D_MODEL), lambda i, j, te, tv: (te[i], jeff(i, j, tv), 0))],
        out_specs=pl.BlockSpec((tm, D_MODEL), lambda i, j, te, tv: (i, 0)),
        scratch_shapes=[pltpu.VMEM((tm, D_MODEL), BF16)])
    vmem = (2 * tm * D_MODEL * 4 + tm * D_MODEL * 2 + 2 * 3 * D_MODEL * tf * 2 + 2 * tm * D_MODEL * 4
            + 4 * tm * tf * 4 + 2 * tm * D_MODEL * 4 + (8 << 20))
    return pl.pallas_call(
        _moe_ffn_kernel, grid_spec=grid_spec, out_shape=jax.ShapeDtypeStruct((P, D_MODEL), F32),
        compiler_params=_params(2, vmem), name="moe_ffn",
    )(tile_expert, tile_valid, xs, wg, wu, wd)


def _moe_combine_kernel(dest_ref, y_hbm, route_ref, x_ref, g_ref, b_ref, o_ref, ob_ref, y0_sc, y1_sc, sem):
    tm = x_ref.shape[0]

    def row_copy(j, k, p):
        dst = y0_sc if k == 0 else y1_sc
        return pltpu.make_async_copy(y_hbm.at[pl.ds(p, 1), :], dst.at[pl.ds(j, 1), :], sem)

    def issue(j, c):
        row_copy(j, 0, dest_ref[0, 0, 2 * j]).start()
        row_copy(j, 1, dest_ref[0, 0, 2 * j + 1]).start()
        return c

    def wait(j, c):
        row_copy(j, 0, 0).wait()
        row_copy(j, 1, 0).wait()
        return c

    lax.fori_loop(0, tm, issue, 0)
    lax.fori_loop(0, tm, wait, 0)
    route = route_ref[...]
    f = route[:, 2:3] * y0_sc[...] + route[:, 3:4] * y1_sc[...]
    out = _layer_norm_rows(DEEPNORM_ALPHA * x_ref[...] + f, g_ref[...], b_ref[...])
    o_ref[...] = out
    ob_ref[...] = out.astype(BF16)


def _moe_combine(y, dest, route, x, g, b):
    T = x.shape[0]
    tm = COMBINE_TM
    row = lambda m: (m, 0)
    const = lambda m: (0, 0)
    vmem = 2 * tm * D_MODEL * 4 + 2 * tm * D_MODEL * 4 + 2 * tm * D_MODEL * 6 + 6 * tm * D_MODEL * 4 + (8 << 20)
    return pl.pallas_call(
        _moe_combine_kernel, grid=(T // tm,),
        in_specs=[pl.BlockSpec((1, 1, 2 * tm), lambda m: (m, 0, 0), memory_space=pltpu.SMEM),
                  pl.BlockSpec(memory_space=pl.ANY),
                  pl.BlockSpec((tm, V7X_LANES), row),
                  pl.BlockSpec((tm, D_MODEL), row),
                  pl.BlockSpec((1, D_MODEL), const), pl.BlockSpec((1, D_MODEL), const)],
        out_specs=[pl.BlockSpec((tm, D_MODEL), row), pl.BlockSpec((tm, D_MODEL), row)],
        out_shape=[jax.ShapeDtypeStruct((T, D_MODEL), F32), jax.ShapeDtypeStruct((T, D_MODEL), BF16)],
        scratch_shapes=[pltpu.VMEM((tm, D_MODEL), F32), pltpu.VMEM((tm, D_MODEL), F32),
                        pltpu.SemaphoreType.DMA(())],
        compiler_params=_params(1, vmem), name="moe_combine",
    )(dest.reshape(T // tm, 1, 2 * tm), y, route, x, g, b)


def _route_plan(route, tm):
    T = route.shape[0]
    n_assign = T * TOP_K
    n_tiles = n_assign // tm + N_EXPERTS
    flat_e = route[:, :TOP_K].astype(jnp.int32).reshape(n_assign)
    onehot = (flat_e[:, None] == jnp.arange(N_EXPERTS, dtype=jnp.int32)[None, :]).astype(jnp.int32)
    csum = jnp.cumsum(onehot, axis=0)
    rank = jnp.sum(csum * onehot, axis=1) - 1
    counts = csum[-1]
    tiles_per = (counts + tm - 1) // tm
    tile_end = jnp.cumsum(tiles_per)
    row_start = (tile_end - tiles_per) * tm
    dest = jnp.sum(onehot * row_start[None, :], axis=1) + rank
    tile_ids = jnp.arange(n_tiles, dtype=jnp.int32)
    n_valid = tile_end[-1]
    tile_valid = (tile_ids < n_valid).astype(jnp.int32)
    owner = jnp.sum((jnp.minimum(tile_ids, n_valid - 1)[:, None] >= tile_end[None, :]).astype(jnp.int32), axis=1)
    tile_expert = jnp.minimum(owner, N_EXPERTS - 1).astype(jnp.int32)
    src = jnp.zeros((n_tiles * tm,), jnp.int32).at[dest].set(jnp.arange(n_assign, dtype=jnp.int32) // TOP_K)
    return dest.reshape(T, TOP_K), src.reshape(n_tiles, 1, tm), tile_expert, tile_valid


def _t5_bucket(rel):
    nb = N_BUCKETS // 2
    max_exact = nb // 2
    n = jnp.abs(rel)
    large = max_exact + (jnp.log(jnp.maximum(n, 1).astype(F32) / max_exact)
                         / math.log(MAX_DISTANCE / max_exact) * (nb - max_exact)).astype(jnp.int32)
    large = jnp.minimum(large, nb - 1)
    return jnp.where(rel > 0, nb, 0) + jnp.where(n < max_exact, n, large)


def _a_bias_table(rel_bias, d):
    i = jnp.arange(A_SUB, dtype=jnp.int32)[:, None]
    j = jnp.arange(A_SUB + 2 * A_HALF, dtype=jnp.int32)[None, :]
    rel = j - A_HALF - i
    bias = rel_bias[_t5_bucket(rel * d)].astype(F32)
    return jnp.where((jnp.abs(rel) <= A_HALF)[:, :, None], bias, NEG_INF).transpose(2, 0, 1)


def _rope_tables(seqs):
    pos = jnp.concatenate([jnp.arange(sl, dtype=jnp.int32) for _, sl in seqs])
    row = (pos // GRID_W).astype(F32)
    col = (pos % GRID_W).astype(F32)
    n_freq = HEAD_DIM // 4
    inv_freq = ROPE_THETA ** (-jnp.arange(n_freq, dtype=F32) / n_freq)
    ang_r = row[:, None] * inv_freq[None, :]
    ang_c = col[:, None] * inv_freq[None, :]
    ang = jnp.concatenate([ang_r, ang_r, ang_c, ang_c], axis=-1)
    sign = jnp.where((jnp.arange(HEAD_DIM) % (HEAD_DIM // 2)) < (HEAD_DIM // 4), -1.0, 1.0).astype(F32)
    return jnp.cos(ang), jnp.sin(ang) * sign[None, :]


def kernel(x_prompt, x_sample, w_in, b_gate, q_norm_g, k_norm_g, rel_bias, w_branch_a, w_branch_b, w_out,
           ln1_g, ln1_b, ln2_g, ln2_b, ffn_w_gate, ffn_w_up, ffn_w_down, router_w, exp_w_gate, exp_w_up,
           exp_w_down):
    seqs = []
    for xs in (x_prompt, x_sample):
        for _ in range(xs.shape[0]):
            seqs.append((sum(sl for _, sl in seqs), xs.shape[1]))
    seqs = tuple(seqs)
    T = sum(sl for _, sl in seqs)
    assert all(s0 % max(PROJ_TM, B_TQ) == 0 and sl % max(PROJ_TM, B_TQ) == 0 for s0, sl in seqs)

    x = jnp.concatenate([x_prompt.reshape(-1, D_MODEL), x_sample.reshape(-1, D_MODEL)], axis=0)
    xb = x.astype(BF16)
    cos, sin_signed = _rope_tables(seqs)
    biases = [_a_bias_table(rel_bias, d) for _, d in DILATED_PATTERNS]
    router_pad = jnp.pad(router_w.astype(F32), ((0, 0), (0, 0), (0, V7X_LANES - N_EXPERTS)))
    router_hi = router_pad.astype(BF16)
    router_lo = (router_pad - router_hi.astype(F32)).astype(BF16)
    router_split = jnp.concatenate([router_hi, router_lo], axis=-1)

    for l in range(DEPTH):
        a1, a4, a16, qb, kb, vb, gates = _in_proj(
            xb, w_in[l].astype(BF16), b_gate[l].reshape(1, 2 * D_MODEL),
            q_norm_g[l].reshape(1, HEAD_DIM), k_norm_g[l].reshape(1, HEAD_DIM), cos, sin_signed)
        o1, l1 = _attn_a(a1, biases[0], 1, seqs)
        o4, l4 = _attn_a(a4, biases[1], 4, seqs)
        o16, l16 = _attn_a(a16, biases[2], 16, seqs)
        ob = _attn_b(qb, kb, vb, seqs)
        moe = l % 2 == 1
        res = _branch(o1, o4, o16, l1, l4, l16, ob, gates, x,
                      w_branch_a[l].astype(BF16), w_branch_b[l].astype(BF16), w_out[l].astype(BF16),
                      ln1_g[l].reshape(1, D_MODEL), ln1_b[l].reshape(1, D_MODEL),
                      router_split[l // 2] if moe else None)
        g2, b2 = ln2_g[l].reshape(1, D_MODEL), ln2_b[l].reshape(1, D_MODEL)
        j = l // 2
        if not moe:
            x1, x1b = res
            x, xb = _ffn_dense(x1b, x1, ffn_w_gate[j].astype(BF16), ffn_w_up[j].astype(BF16),
                               ffn_w_down[j].astype(BF16), g2, b2)
        else:
            x1, x1b, route = res
            dest, src, tile_expert, tile_valid = _route_plan(route, MOE_TM)
            xs = _moe_gather(x1, src)
            ys = _moe_ffn(xs, tile_expert, tile_valid, exp_w_gate[j].astype(BF16), exp_w_up[j].astype(BF16),
                          exp_w_down[j].astype(BF16))
            x, xb = _moe_combine(ys, dest, route, x1, g2, b2)

    n_prompt = x_prompt.shape[0] * x_prompt.shape[1]
    return (x[:n_prompt].reshape(x_prompt.shape), x[n_prompt:].reshape(x_sample.shape))
```

```python
import functools
import math

import numpy as np
import jax
import jax.numpy as jnp
from jax import lax
from jax.experimental import pallas as pl
from jax.experimental.pallas import tpu as pltpu

D_MODEL = 2048
DEPTH = 2
HEAD_DIM = 128
A_HEADS = 8
A_WIDTH = A_HEADS * HEAD_DIM
DILATED_PATTERNS = ((128, 1), (512, 4), (2048, 16))
A_HALF = 64
B_HEADS = 8
B_KV_HEADS = 2
B_GROUPS = B_HEADS // B_KV_HEADS
B_WIDTH = B_HEADS * HEAD_DIM
B_KV_WIDTH = B_KV_HEADS * HEAD_DIM
GRID_W = 64
ROPE_THETA = 10000.0
N_BUCKETS = 32
MAX_DISTANCE = 1024
IN_COLS = 3 * A_WIDTH + B_WIDTH + 2 * B_KV_WIDTH + 2 * D_MODEL
D_FF = 5632
N_EXPERTS = 8
TOP_K = 2
D_FF_EXPERT = 7168
DEEPNORM_ALPHA = (2.0 * DEPTH) ** 0.25
LN_EPS = 1e-5
RMS_EPS = 1e-6
NEG_INF = -1e30
ATTN_SCALE = HEAD_DIM ** -0.5

V7X_LANES = 128
V7X_VMEM_BYTES = 64 * 1024 * 1024
ROW_SLAB = D_MODEL // V7X_LANES

PROJ_TM = 1024
PROJ_TN = 512
A_BLOCK = 256
A_SUB = 128
B_TQ = 1024
B_TK = 4096
B_CK = 512
BRANCH_TM = 256
FFN_TM = 1024
FFN_TF = 512
MOE_TM = 512
MOE_TF = 512
COMBINE_TM = 256

BF16 = jnp.bfloat16
F32 = jnp.float32

assert all(w // (2 * d) == A_HALF for w, d in DILATED_PATTERNS)
assert IN_COLS % PROJ_TN == 0 and D_FF % FFN_TF == 0 and D_FF_EXPERT % MOE_TF == 0


def _vmem_limit(nbytes):
    return int(min(nbytes, V7X_VMEM_BYTES - 4 * 1024 * 1024))


def _params(n_axes, vmem_bytes):
    return pltpu.CompilerParams(
        dimension_semantics=("arbitrary",) * n_axes, vmem_limit_bytes=_vmem_limit(vmem_bytes))


def _any_eq(b, values):
    return functools.reduce(jnp.logical_or, [b == v for v in values])


def _layer_norm_rows(z, g, b):
    mu = jnp.mean(z, axis=-1, keepdims=True)
    zc = z - mu
    var = jnp.mean(zc * zc, axis=-1, keepdims=True)
    return zc * lax.rsqrt(var + LN_EPS) * g + b


_N_QA, _N_KA, _N_VA = 0, A_WIDTH // PROJ_TN, 2 * A_WIDTH // PROJ_TN
_N_QB = 3 * A_WIDTH // PROJ_TN
_N_KVB = _N_QB + B_WIDTH // PROJ_TN
_N_GATE = _N_KVB + 1
_N_TILES = IN_COLS // PROJ_TN
assert 2 * B_KV_WIDTH == PROJ_TN and A_WIDTH % PROJ_TN == 0


def _rms_rope_store(acc, n_heads, g, cos, sin_signed, post_scale, out_ref):
    lane = lax.broadcasted_iota(jnp.int32, (acc.shape[0], HEAD_DIM), 1)
    low = (lane % (HEAD_DIM // 2)) < (HEAD_DIM // 4)
    for h in range(n_heads):
        xh = acc[:, h * HEAD_DIM:(h + 1) * HEAD_DIM]
        y = xh * lax.rsqrt(jnp.mean(xh * xh, axis=-1, keepdims=True) + RMS_EPS) * g
        rot = jnp.where(low, pltpu.roll(y, HEAD_DIM - HEAD_DIM // 4, 1), pltpu.roll(y, HEAD_DIM // 4, 1))
        roped = y * cos + rot * sin_signed
        if post_scale != 1.0:
            roped = roped * post_scale
        out_ref[:, h * HEAD_DIM:(h + 1) * HEAD_DIM] = roped.astype(out_ref.dtype)


B_Q_PRESCALE = ATTN_SCALE * math.log2(math.e)


def _in_proj_kernel(x_ref, w_ref, bg_ref, qg_ref, kg_ref, cos_ref, sin_ref,
                    a1_ref, a4_ref, a16_ref, qb_ref, kb_ref, vb_ref, gate_ref, acc_ref):
    n = pl.program_id(1)
    tm = x_ref.shape[0]

    def product():
        return jnp.dot(x_ref[...], w_ref[...], preferred_element_type=F32)

    @pl.when(n < _N_QB)
    def _():
        acc = product()
        a1_ref[0] = acc.astype(a1_ref.dtype)
        for c in range(acc_ref.shape[0]):
            cs = slice(c * V7X_LANES, (c + 1) * V7X_LANES)
            acc_ref[c] = acc[:, cs]
            for d, o in ((4, a4_ref), (16, a16_ref)):
                for r in range(d):
                    o[r, :, cs] = acc_ref[c, pl.ds(r, tm // d, stride=d), :].astype(o.dtype)

    @pl.when(jnp.logical_and(n >= _N_QB, n < _N_KVB))
    def _():
        _rms_rope_store(product(), PROJ_TN // HEAD_DIM, qg_ref[...], cos_ref[...], sin_ref[...], B_Q_PRESCALE,
                        qb_ref)

    @pl.when(n == _N_KVB)
    def _():
        acc = product()
        _rms_rope_store(acc[:, :B_KV_WIDTH], B_KV_HEADS, kg_ref[...], cos_ref[...], sin_ref[...], 1.0, kb_ref)
        vb_ref[...] = acc[:, B_KV_WIDTH:].astype(vb_ref.dtype)

    @pl.when(n >= _N_GATE)
    def _():
        gate_ref[...] = jax.nn.sigmoid(product() + bg_ref[...]).astype(gate_ref.dtype)


def _in_proj(xb, w, bg, qg, kg, cos, sin_signed):
    T = xb.shape[0]
    tm, tn = PROJ_TM, PROJ_TN
    grid = (T // tm, _N_TILES)
    a_tiles = A_WIDTH // tn

    def clampn(lo, cnt):
        return lambda m, n: (m, jnp.clip(n - lo, 0, cnt - 1))

    def dil_spec(d):
        def index(m, n):
            nc = jnp.minimum(n, 3 * a_tiles - 1)
            return (nc // a_tiles, 0, m, nc % a_tiles)
        return pl.BlockSpec((None, d, tm // d, tn), index)

    def dil_shape(d):
        return jax.ShapeDtypeStruct((3, d, T // d, A_WIDTH), BF16)

    in_specs = [
        pl.BlockSpec((tm, D_MODEL), lambda m, n: (m, 0)),
        pl.BlockSpec((D_MODEL, tn), lambda m, n: (0, n)),
        pl.BlockSpec((1, tn), lambda m, n: (0, jnp.clip(n - _N_GATE, 0, 2 * D_MODEL // tn - 1))),
        pl.BlockSpec((1, HEAD_DIM), lambda m, n: (0, 0)),
        pl.BlockSpec((1, HEAD_DIM), lambda m, n: (0, 0)),
        pl.BlockSpec((tm, HEAD_DIM), lambda m, n: (m, 0)),
        pl.BlockSpec((tm, HEAD_DIM), lambda m, n: (m, 0)),
    ]
    out_specs = [dil_spec(1), dil_spec(4), dil_spec(16),
                 pl.BlockSpec((tm, tn), clampn(_N_QB, B_WIDTH // tn)),
                 pl.BlockSpec((tm, B_KV_WIDTH), lambda m, n: (m, 0)),
                 pl.BlockSpec((tm, B_KV_WIDTH), lambda m, n: (m, 0)),
                 pl.BlockSpec((tm, tn), clampn(_N_GATE, 2 * D_MODEL // tn))]
    out_shape = [dil_shape(1), dil_shape(4), dil_shape(16),
                 jax.ShapeDtypeStruct((T, B_WIDTH), BF16),
                 jax.ShapeDtypeStruct((T, B_KV_WIDTH), BF16),
                 jax.ShapeDtypeStruct((T, B_KV_WIDTH), BF16),
                 jax.ShapeDtypeStruct((T, 2 * D_MODEL), BF16)]
    vmem = (2 * tm * D_MODEL * 2 + 2 * D_MODEL * tn * 2 + 7 * 2 * tm * tn * 2 + 4 * tm * HEAD_DIM * 4
            + 6 * tm * tn * 4 + (8 << 20))
    return pl.pallas_call(
        _in_proj_kernel, grid=grid, in_specs=in_specs, out_specs=out_specs, out_shape=out_shape,
        scratch_shapes=[pltpu.VMEM((tn // V7X_LANES, tm, V7X_LANES), F32)],
        compiler_params=_params(2, vmem), name="in_proj",
    )(xb, w, bg, qg, kg, cos, sin_signed)


def _attn_a_kernel(q_ref, kp_ref, kc_ref, kn_ref, vp_ref, vc_ref, vn_ref, bias_ref, o_ref, lse_ref,
                   *, first_blocks, last_blocks):
    b = pl.program_id(1)
    win = A_SUB + 2 * A_HALF
    is_first = _any_eq(b, first_blocks)
    is_last = _any_eq(b, last_blocks)
    col = lax.broadcasted_iota(jnp.int32, (A_SUB, win), 1)
    lane = lax.broadcasted_iota(jnp.int32, (A_SUB, V7X_LANES), 1)
    n_sub = A_BLOCK // A_SUB

    def window(prev_ref, cur_ref, next_ref, start, hs):
        parts = []
        if start < 0:
            parts.append(prev_ref[A_BLOCK + start:, hs])
        parts.append(cur_ref[max(start, 0):min(start + win, A_BLOCK), hs])
        if start + win > A_BLOCK:
            parts.append(next_ref[:start + win - A_BLOCK, hs])
        return parts[0] if len(parts) == 1 else jnp.concatenate(parts, axis=0)

    for sb in range(n_sub):
        start = sb * A_SUB - A_HALF
        rows = slice(sb * A_SUB, (sb + 1) * A_SUB)
        lo = jnp.where(is_first, -start, 0) if start < 0 else 0
        hi = jnp.where(is_last, A_BLOCK - start, win) if start + win > A_BLOCK else win
        outside = jnp.logical_or(col < lo, col >= hi)
        lse_tile = jnp.zeros((A_SUB, V7X_LANES), F32)
        for h in range(A_HEADS):
            hs = slice(h * HEAD_DIM, (h + 1) * HEAD_DIM)
            q = q_ref[rows, hs]
            k = window(kp_ref, kc_ref, kn_ref, start, hs)
            v = window(vp_ref, vc_ref, vn_ref, start, hs)
            s = lax.dot_general(q, k, (((1,), (1,)), ((), ())), preferred_element_type=F32)
            s = jnp.where(outside, NEG_INF, s * ATTN_SCALE + bias_ref[h])
            m = jnp.max(s, axis=1, keepdims=True)
            p = jnp.exp(s - m)
            l = jnp.sum(p, axis=1, keepdims=True)
            o = jnp.dot(p.astype(v.dtype), v, preferred_element_type=F32)
            o_ref[rows, hs] = o / l
            lse_tile = jnp.where(lane == h, m + jnp.log(l), lse_tile)
        lse_ref[rows, :] = lse_tile


def _attn_a(a, bias, d, seqs):
    _, _, L, _ = a.shape
    nb = L // A_BLOCK
    assert all(s0 % (d * A_BLOCK) == 0 and sl % (d * A_BLOCK) == 0 for s0, sl in seqs)
    first_blocks = tuple(s0 // d // A_BLOCK for s0, _ in seqs)
    last_blocks = tuple((s0 + sl) // d // A_BLOCK - 1 for s0, sl in seqs)

    def cur(which):
        return lambda r, b: (which, r, b, 0)

    def prev(which):
        return lambda r, b: (which, r, jnp.where(_any_eq(b, first_blocks), b, b - 1), 0)

    def nxt(which):
        return lambda r, b: (which, r, jnp.where(_any_eq(b, last_blocks), b, b + 1), 0)

    blk = (None, None, A_BLOCK, A_WIDTH)
    win = A_SUB + 2 * A_HALF
    in_specs = [pl.BlockSpec(blk, cur(0)),
                pl.BlockSpec(blk, prev(1)), pl.BlockSpec(blk, cur(1)), pl.BlockSpec(blk, nxt(1)),
                pl.BlockSpec(blk, prev(2)), pl.BlockSpec(blk, cur(2)), pl.BlockSpec(blk, nxt(2)),
                pl.BlockSpec((A_HEADS, A_SUB, win), lambda r, b: (0, 0, 0))]
    out_idx = lambda r, b: (r, b, 0)
    out_specs = [pl.BlockSpec((None, A_BLOCK, A_WIDTH), out_idx), pl.BlockSpec((None, A_BLOCK, V7X_LANES), out_idx)]
    out_shape = [jax.ShapeDtypeStruct((d, L, A_WIDTH), F32), jax.ShapeDtypeStruct((d, L, V7X_LANES), F32)]
    vmem = 7 * 2 * A_BLOCK * A_WIDTH * 2 + 2 * A_HEADS * A_SUB * win * 4 + 2 * A_BLOCK * A_WIDTH * 4 + (8 << 20)
    return pl.pallas_call(
        functools.partial(_attn_a_kernel, first_blocks=first_blocks, last_blocks=last_blocks),
        grid=(d, nb), in_specs=in_specs, out_specs=out_specs, out_shape=out_shape,
        compiler_params=_params(2, vmem), name=f"attn_a_d{d}",
    )(a, a, a, a, a, a, a, bias)


def _attn_b_kernel(qi_ref, ki_ref, hi_ref, fl_ref, q_ref, k_ref, v_ref, o_ref, m_sc, l_sc, acc_sc):
    step = pl.program_id(0)
    flags = fl_ref[step]

    @pl.when((flags & 1) != 0)
    def _():
        m_sc[...] = jnp.full(m_sc.shape, NEG_INF, F32)
        l_sc[...] = jnp.zeros(l_sc.shape, F32)
        acc_sc[...] = jnp.zeros(acc_sc.shape, F32)


    tq = q_ref.shape[0]
    q = jnp.concatenate([q_ref[:, g * HEAD_DIM:(g + 1) * HEAD_DIM] for g in range(B_GROUPS)], axis=0)

    n_chunks = k_ref.shape[0] // B_CK

    def chunk_rows(ci):
        return pl.ds(pl.multiple_of(ci * B_CK, B_CK), B_CK)

    def scores(ci):
        return lax.dot_general(q, k_ref[chunk_rows(ci), :], (((1,), (1,)), ((), ())),
                               preferred_element_type=F32)

    def softmax_pv(ci, t):
        v = v_ref[chunk_rows(ci), :]
        m_prev = m_sc[...]
        m_new = jnp.maximum(m_prev, jnp.max(t, axis=1, keepdims=True))
        alpha = jnp.exp2(m_prev - m_new)
        p = jnp.exp2(t - jnp.concatenate([m_new] * (B_CK // V7X_LANES), axis=1))
        l_sc[...] = alpha * l_sc[...] + jnp.sum(p, axis=1, keepdims=True)
        acc_sc[...] = alpha * acc_sc[...] + jnp.dot(p.astype(v.dtype), v, preferred_element_type=F32)
        m_sc[...] = m_new

    def chunk(ci, carry):
        softmax_pv(ci, scores(ci))
        return carry

    lax.fori_loop(0, n_chunks, chunk, 0)

    @pl.when((flags & 2) != 0)
    def _():
        for g in range(B_GROUPS):
            rows = slice(g * tq, (g + 1) * tq)
            o_ref[:, g * HEAD_DIM:(g + 1) * HEAD_DIM] = (acc_sc[rows, :] / l_sc[rows, :]).astype(o_ref.dtype)


def _attn_b_schedule(seqs, tk):
    qi, ki, hi, fl = [], [], [], []
    for s0, sl in seqs:
        nkv = sl // tk
        for h in range(B_KV_HEADS):
            for qb in range(sl // B_TQ):
                for kb in range(nkv):
                    qi.append(s0 // B_TQ + qb)
                    ki.append(s0 // tk + kb)
                    hi.append(h)
                    fl.append((1 if kb == 0 else 0) | (2 if kb == nkv - 1 else 0))
    return [np.asarray(a, np.int32) for a in (qi, ki, hi, fl)]


def _attn_b(qb, kb, vb, seqs):
    T = qb.shape[0]
    tk = min([B_TK] + [sl for _, sl in seqs])
    assert all(sl % tk == 0 and s0 % tk == 0 and sl % B_TQ == 0 for s0, sl in seqs)
    qi, ki, hi, fl = _attn_b_schedule(seqs, tk)
    gw = B_GROUPS * HEAD_DIM
    rows = B_GROUPS * B_TQ
    grid_spec = pltpu.PrefetchScalarGridSpec(
        num_scalar_prefetch=4, grid=(len(qi),),
        in_specs=[pl.BlockSpec((B_TQ, gw), lambda s, qi, ki, hi, fl: (qi[s], hi[s])),
                  pl.BlockSpec((tk, HEAD_DIM), lambda s, qi, ki, hi, fl: (ki[s], hi[s])),
                  pl.BlockSpec((tk, HEAD_DIM), lambda s, qi, ki, hi, fl: (ki[s], hi[s]))],
        out_specs=pl.BlockSpec((B_TQ, gw), lambda s, qi, ki, hi, fl: (qi[s], hi[s])),
        scratch_shapes=[pltpu.VMEM((rows, V7X_LANES), F32)] * 3)
    vmem = 4 * B_TQ * gw * 2 + 4 * tk * HEAD_DIM * 2 + 3 * rows * V7X_LANES * 4 + 4 * rows * B_CK * 4 + (8 << 20)
    return pl.pallas_call(
        _attn_b_kernel, grid_spec=grid_spec, out_shape=jax.ShapeDtypeStruct((T, B_WIDTH), BF16),
        compiler_params=_params(1, vmem), name="attn_b",
    )(jnp.asarray(qi), jnp.asarray(ki), jnp.asarray(hi), jnp.asarray(fl), qb, kb, vb)


def _branch_kernel(*refs, with_router):
    (o1_ref, o4_ref, o16_ref, l1_ref, l4_ref, l16_ref, ob_ref, sga_ref, sgb_ref, x_ref,
     wba_ref, wbb_ref, wo_ref, g_ref, b_ref) = refs[:15]
    rest = refs[15:]
    if with_router:
        rw_ref, x1_ref, x1c_ref, route_ref, o4_sc, o16_sc, l4_sc, l16_sc = rest
    else:
        x1_ref, x1b_ref, o4_sc, o16_sc, l4_sc, l16_sc = rest
    tm = x_ref.shape[0]
    for d, src, dst, lsrc, ldst in ((4, o4_ref, o4_sc, l4_ref, l4_sc), (16, o16_ref, o16_sc, l16_ref, l16_sc)):
        for r in range(d):
            ldst[pl.ds(r, tm // d, stride=d), :] = lsrc[r]
            for h in range(A_HEADS):
                dst[h, pl.ds(r, tm // d, stride=d), :] = src[r, :, h * HEAD_DIM:(h + 1) * HEAD_DIM]
    l1, l4, l16 = l1_ref[0], l4_sc[...], l16_sc[...]
    mx = jnp.maximum(jnp.maximum(l1, l4), l16)
    e1, e4, e16 = jnp.exp(l1 - mx), jnp.exp(l4 - mx), jnp.exp(l16 - mx)
    den = e1 + e4 + e16
    w1, w4, w16 = e1 / den, e4 / den, e16 / den
    parts = []
    for h in range(A_HEADS):
        hs = slice(h * HEAD_DIM, (h + 1) * HEAD_DIM)
        oa_h = w1[:, h:h + 1] * o1_ref[0, :, hs] + w4[:, h:h + 1] * o4_sc[h] + w16[:, h:h + 1] * o16_sc[h]
        parts.append(oa_h.astype(BF16))
    oa = jnp.concatenate(parts, axis=1)
    ya = jnp.dot(oa, wba_ref[...], preferred_element_type=F32)
    yb = jnp.dot(ob_ref[...], wbb_ref[...], preferred_element_type=F32)
    merged = sga_ref[...].astype(F32) * ya + sgb_ref[...].astype(F32) * yb
    y = jnp.dot(merged.astype(BF16), wo_ref[...], preferred_element_type=F32)
    out = _layer_norm_rows(DEEPNORM_ALPHA * x_ref[...] + y, g_ref[...], b_ref[...])
    x1_ref[...] = out
    if not with_router:
        x1b_ref[...] = out.astype(BF16)
    if with_router:
        for c in range(ROW_SLAB):
            x1c_ref[pl.ds(c, tm, stride=ROW_SLAB), :] = out[:, c * V7X_LANES:(c + 1) * V7X_LANES]
        out_hi = out.astype(BF16)
        out_lo = (out - out_hi.astype(F32)).astype(BF16)
        both = jnp.dot(out_hi, rw_ref[...], preferred_element_type=F32)
        logits = (both[:, :V7X_LANES] + both[:, V7X_LANES:]
                  + jnp.dot(out_lo, rw_ref[:, :V7X_LANES], preferred_element_type=F32))
        lane = lax.broadcasted_iota(jnp.int32, logits.shape, 1)
        logits = jnp.where(lane < N_EXPERTS, logits, -jnp.inf)
        v1 = jnp.max(logits, axis=1, keepdims=True)
        i1 = jnp.min(jnp.where(logits == v1, lane, V7X_LANES), axis=1, keepdims=True)
        rem = jnp.where(lane == i1, -jnp.inf, logits)
        v2 = jnp.max(rem, axis=1, keepdims=True)
        i2 = jnp.min(jnp.where(rem == v2, lane, V7X_LANES), axis=1, keepdims=True)
        e2 = jnp.exp(v2 - v1)
        g1 = 1.0 / (1.0 + e2)
        g2 = e2 / (1.0 + e2)
        route = jnp.where(lane == 0, i1.astype(F32),
                          jnp.where(lane == 1, i2.astype(F32),
                                    jnp.where(lane == 2, g1, jnp.where(lane == 3, g2, 0.0))))
        route_ref[...] = route


def _branch(o1, o4, o16, l1, l4, l16, ob, gates, x, wba, wbb, wo, g, b, router_w=None):
    T = x.shape[0]
    tm = BRANCH_TM
    with_router = router_w is not None
    row = lambda m: (m, 0)
    const = lambda m: (0, 0)
    single = pl.Buffered(1)
    in_specs = [
        pl.BlockSpec((1, tm, A_WIDTH), lambda m: (0, m, 0)),
        pl.BlockSpec((4, tm // 4, A_WIDTH), lambda m: (0, m, 0)),
        pl.BlockSpec((16, tm // 16, A_WIDTH), lambda m: (0, m, 0)),
        pl.BlockSpec((1, tm, V7X_LANES), lambda m: (0, m, 0)),
        pl.BlockSpec((4, tm // 4, V7X_LANES), lambda m: (0, m, 0)),
        pl.BlockSpec((16, tm // 16, V7X_LANES), lambda m: (0, m, 0)),
        pl.BlockSpec((tm, B_WIDTH), row),
        pl.BlockSpec((tm, D_MODEL), lambda m: (m, 0)),
        pl.BlockSpec((tm, D_MODEL), lambda m: (m, 1)),
        pl.BlockSpec((tm, D_MODEL), row),
        pl.BlockSpec((A_WIDTH, D_MODEL), const, pipeline_mode=single),
        pl.BlockSpec((B_WIDTH, D_MODEL), const, pipeline_mode=single),
        pl.BlockSpec((D_MODEL, D_MODEL), const, pipeline_mode=single),
        pl.BlockSpec((1, D_MODEL), const),
        pl.BlockSpec((1, D_MODEL), const),
    ]
    args = [o1, o4, o16, l1, l4, l16, ob, gates, gates, x, wba, wbb, wo, g, b]
    out_specs = [pl.BlockSpec((tm, D_MODEL), row)]
    out_shape = [jax.ShapeDtypeStruct((T, D_MODEL), F32)]
    if with_router:
        in_specs.append(pl.BlockSpec((D_MODEL, 2 * V7X_LANES), const, pipeline_mode=single))
        args.append(router_w)
        out_specs += [pl.BlockSpec((tm * ROW_SLAB, V7X_LANES), row), pl.BlockSpec((tm, V7X_LANES), row)]
        out_shape += [jax.ShapeDtypeStruct((T * ROW_SLAB, V7X_LANES), F32),
                      jax.ShapeDtypeStruct((T, V7X_LANES), F32)]
    else:
        out_specs.append(pl.BlockSpec((tm, D_MODEL), row))
        out_shape.append(jax.ShapeDtypeStruct((T, D_MODEL), BF16))
    scratch = [pltpu.VMEM((A_HEADS, tm, HEAD_DIM), F32), pltpu.VMEM((A_HEADS, tm, HEAD_DIM), F32),
               pltpu.VMEM((tm, V7X_LANES), F32), pltpu.VMEM((tm, V7X_LANES), F32)]
    vmem = ((A_WIDTH + B_WIDTH + D_MODEL) * D_MODEL * 2 + D_MODEL * V7X_LANES * 4
            + 2 * 3 * tm * A_WIDTH * 4 + 2 * tm * A_WIDTH * 4 + 2 * tm * B_WIDTH * 2 + 4 * tm * D_MODEL * 2
            + 2 * tm * D_MODEL * 4 + 2 * tm * D_MODEL * 6 + 8 * tm * D_MODEL * 4 + (8 << 20))
    return pl.pallas_call(
        functools.partial(_branch_kernel, with_router=with_router),
        grid=(T // tm,), in_specs=in_specs, out_specs=out_specs, out_shape=out_shape,
        scratch_shapes=scratch, compiler_params=_params(1, vmem),
        name="branch_router" if with_router else "branch",
    )(*args)


def _swiglu_accumulate(xb, wg_ref, wu_ref, wd_ref, acc_ref):
    g = jnp.dot(xb, wg_ref[...], preferred_element_type=F32)
    u = jnp.dot(xb, wu_ref[...], preferred_element_type=F32)
    a = (g * jax.nn.sigmoid(g) * u).astype(BF16)
    acc_ref[...] += jnp.dot(a, wd_ref[...], preferred_element_type=F32)


def _ffn_dense_kernel(xb_ref, x_ref, wg_ref, wu_ref, wd_ref, g_ref, b_ref, o_ref, ob_ref):
    j = pl.program_id(1)

    @pl.when(j == 0)
    def _():
        o_ref[...] = jnp.zeros(o_ref.shape, F32)

    _swiglu_accumulate(xb_ref[...], wg_ref, wu_ref, wd_ref, o_ref)

    @pl.when(j == pl.num_programs(1) - 1)
    def _():
        out = _layer_norm_rows(DEEPNORM_ALPHA * x_ref[...] + o_ref[...], g_ref[...], b_ref[...])
        o_ref[...] = out
        ob_ref[...] = out.astype(BF16)


def _ffn_dense(xb, x, wg, wu, wd, g, b):
    T = x.shape[0]
    tm, tf = FFN_TM, FFN_TF
    row = lambda m, j: (m, 0)
    const = lambda m, j: (0, 0)
    single = pl.Buffered(1)
    in_specs = [pl.BlockSpec((tm, D_MODEL), row), pl.BlockSpec((tm, D_MODEL), row, pipeline_mode=single),
                pl.BlockSpec((D_MODEL, tf), lambda m, j: (0, j)),
                pl.BlockSpec((D_MODEL, tf), lambda m, j: (0, j)),
                pl.BlockSpec((tf, D_MODEL), lambda m, j: (j, 0)),
                pl.BlockSpec((1, D_MODEL), const), pl.BlockSpec((1, D_MODEL), const)]
    out_specs = [pl.BlockSpec((tm, D_MODEL), row, pipeline_mode=single),
                 pl.BlockSpec((tm, D_MODEL), row, pipeline_mode=single)]
    out_shape = [jax.ShapeDtypeStruct((T, D_MODEL), F32), jax.ShapeDtypeStruct((T, D_MODEL), BF16)]
    vmem = (2 * tm * D_MODEL * 2 + tm * D_MODEL * 4 + 2 * 3 * D_MODEL * tf * 2 + tm * D_MODEL * 6
            + 4 * tm * tf * 4 + 2 * tm * D_MODEL * 4 + (4 << 20))
    return pl.pallas_call(
        _ffn_dense_kernel, grid=(T // tm, D_FF // tf), in_specs=in_specs, out_specs=out_specs,
        out_shape=out_shape, compiler_params=_params(2, vmem), name="ffn_dense",
    )(xb, x, wg, wu, wd, g, b)


def _moe_gather_kernel(src_ref, x_hbm, o_ref, sem):
    tm = src_ref.shape[2]

    def slab_copy(j, t):
        return pltpu.make_async_copy(x_hbm.at[pl.ds(t * ROW_SLAB, ROW_SLAB), :],
                                     o_ref.at[pl.ds(j * ROW_SLAB, ROW_SLAB), :], sem)

    def issue(j, c):
        slab_copy(j, src_ref[0, 0, j]).start()
        return c

    def wait(j, c):
        slab_copy(j, 0).wait()
        return c

    lax.fori_loop(0, tm, issue, 0, unroll=8)
    lax.fori_loop(0, tm, wait, 0, unroll=8)


def _moe_gather(xc, src):
    n_tiles, _, tm = src.shape
    vmem = 2 * tm * D_MODEL * 4 + (8 << 20)
    return pl.pallas_call(
        _moe_gather_kernel, grid=(n_tiles,),
        in_specs=[pl.BlockSpec((1, 1, tm), lambda i: (i, 0, 0), memory_space=pltpu.SMEM),
                  pl.BlockSpec(memory_space=pl.ANY)],
        out_specs=pl.BlockSpec((tm * ROW_SLAB, V7X_LANES), lambda i: (i, 0)),
        out_shape=jax.ShapeDtypeStruct((n_tiles * tm * ROW_SLAB, V7X_LANES), F32),
        scratch_shapes=[pltpu.SemaphoreType.DMA(())],
        compiler_params=_params(1, vmem), name="moe_gather",
    )(src, xc)


def _moe_ffn_kernel(te_ref, tv_ref, x_ref, wg_ref, wu_ref, wd_ref, o_ref, xb_sc):
    i = pl.program_id(0)
    j = pl.program_id(1)
    valid = tv_ref[i] != 0
    tm = o_ref.shape[0]

    @pl.when(j == 0)
    def _():
        o_ref[...] = jnp.zeros(o_ref.shape, F32)
        for c in range(ROW_SLAB):
            cs = slice(c * V7X_LANES, (c + 1) * V7X_LANES)
            xb_sc[:, cs] = x_ref[pl.ds(c, tm, stride=ROW_SLAB), :].astype(BF16)

    @pl.when(valid)
    def _():
        _swiglu_accumulate(xb_sc[...], wg_ref, wu_ref, wd_ref, o_ref)


def _moe_ffn(xs, tile_expert, tile_valid, wg, wu, wd):
    P = xs.shape[0] // ROW_SLAB
    tm, tf = MOE_TM, MOE_TF
    nj = D_FF_EXPERT // tf

    def jeff(i, j, tv):
        return jnp.where(tv[i] != 0, j, nj - 1)

    grid_spec = pltpu.PrefetchScalarGridSpec(
        num_scalar_prefetch=2, grid=(P // tm, nj),
        in_specs=[pl.BlockSpec((tm * ROW_SLAB, V7X_LANES), lambda i, j, te, tv: (i, 0)),
                  pl.BlockSpec((None, D_MODEL, tf), lambda i, j, te, tv: (te[i], 0, jeff(i, j, tv))),
                  pl.BlockSpec((None, D_MODEL, tf), lambda i, j, te, tv: (te[i], 0, jeff(i, j, tv))),
                  pl.BlockSpec((None, tf, D_MODEL), lambda i, j, te, tv: (te[i], jeff(i, j, tv), 0))],
        out_specs=pl.BlockSpec((tm, D_MODEL), lambda i, j, te, tv: (i, 0)),
        scratch_shapes=[pltpu.VMEM((tm, D_MODEL), BF16)])
    vmem = (2 * tm * D_MODEL * 4 + tm * D_MODEL * 2 + 2 * 3 * D_MODEL * tf * 2 + 2 * tm * D_MODEL * 4
            + 4 * tm * tf * 4 + 2 * tm * D_MODEL * 4 + (8 << 20))
    return pl.pallas_call(
        _moe_ffn_kernel, grid_spec=grid_spec, out_shape=jax.ShapeDtypeStruct((P, D_MODEL), F32),
        compiler_params=_params(2, vmem), name="moe_ffn",
    )(tile_expert, tile_valid, xs, wg, wu, wd)


def _moe_combine_kernel(dest_ref, y_hbm, route_ref, x_ref, g_ref, b_ref, o_ref, ob_ref, y0_sc, y1_sc, sem):
    tm = x_ref.shape[0]

    def row_copy(j, k, p):
        dst = y0_sc if k == 0 else y1_sc
        return pltpu.make_async_copy(y_hbm.at[pl.ds(p, 1), :], dst.at[pl.ds(j, 1), :], sem)

    def issue(j, c):
        row_copy(j, 0, dest_ref[0, 0, 2 * j]).start()
        row_copy(j, 1, dest_ref[0, 0, 2 * j + 1]).start()
        return c

    def wait(j, c):
        row_copy(j, 0, 0).wait()
        row_copy(j, 1, 0).wait()
        return c

    lax.fori_loop(0, tm, issue, 0)
    lax.fori_loop(0, tm, wait, 0)
    route = route_ref[...]
    f = route[:, 2:3] * y0_sc[...] + route[:, 3:4] * y1_sc[...]
    out = _layer_norm_rows(DEEPNORM_ALPHA * x_ref[...] + f, g_ref[...], b_ref[...])
    o_ref[...] = out
    ob_ref[...] = out.astype(BF16)


def _moe_combine(y, dest, route, x, g, b):
    T = x.shape[0]
    tm = COMBINE_TM
    row = lambda m: (m, 0)
    const = lambda m: (0, 0)
    vmem = 2 * tm * D_MODEL * 4 + 2 * tm * D_MODEL * 4 + 2 * tm * D_MODEL * 6 + 6 * tm * D_MODEL * 4 + (8 << 20)
    return pl.pallas_call(
        _moe_combine_kernel, grid=(T // tm,),
        in_specs=[pl.BlockSpec((1, 1, 2 * tm), lambda m: (m, 0, 0), memory_space=pltpu.SMEM),
                  pl.BlockSpec(memory_space=pl.ANY),
                  pl.BlockSpec((tm, V7X_LANES), row),
                  pl.BlockSpec((tm, D_MODEL), row),
                  pl.BlockSpec((1, D_MODEL), const), pl.BlockSpec((1, D_MODEL), const)],
        out_specs=[pl.BlockSpec((tm, D_MODEL), row), pl.BlockSpec((tm, D_MODEL), row)],
        out_shape=[jax.ShapeDtypeStruct((T, D_MODEL), F32), jax.ShapeDtypeStruct((T, D_MODEL), BF16)],
        scratch_shapes=[pltpu.VMEM((tm, D_MODEL), F32), pltpu.VMEM((tm, D_MODEL), F32),
                        pltpu.SemaphoreType.DMA(())],
        compiler_params=_params(1, vmem), name="moe_combine",
    )(dest.reshape(T // tm, 1, 2 * tm), y, route, x, g, b)


def _route_plan(route, tm):
    T = route.shape[0]
    n_assign = T * TOP_K
    n_tiles = n_assign // tm + N_EXPERTS
    flat_e = route[:, :TOP_K].astype(jnp.int32).reshape(n_assign)
    onehot = (flat_e[:, None] == jnp.arange(N_EXPERTS, dtype=jnp.int32)[None, :]).astype(jnp.int32)
    csum = jnp.cumsum(onehot, axis=0)
    rank = jnp.sum(csum * onehot, axis=1) - 1
    counts = csum[-1]
    tiles_per = (counts + tm - 1) // tm
    tile_end = jnp.cumsum(tiles_per)
    row_start = (tile_end - tiles_per) * tm
    dest = jnp.sum(onehot * row_start[None, :], axis=1) + rank
    tile_ids = jnp.arange(n_tiles, dtype=jnp.int32)
    n_valid = tile_end[-1]
    tile_valid = (tile_ids < n_valid).astype(jnp.int32)
    owner = jnp.sum((jnp.minimum(tile_ids, n_valid - 1)[:, None] >= tile_end[None, :]).astype(jnp.int32), axis=1)
    tile_expert = jnp.minimum(owner, N_EXPERTS - 1).astype(jnp.int32)
    src = jnp.zeros((n_tiles * tm,), jnp.int32).at[dest].set(jnp.arange(n_assign, dtype=jnp.int32) // TOP_K)
    return dest.reshape(T, TOP_K), src.reshape(n_tiles, 1, tm), tile_expert, tile_valid


def _t5_bucket(rel):
    nb = N_BUCKETS // 2
    max_exact = nb // 2
    n = jnp.abs(rel)
    large = max_exact + (jnp.log(jnp.maximum(n, 1).astype(F32) / max_exact)
                         / math.log(MAX_DISTANCE / max_exact) * (nb - max_exact)).astype(jnp.int32)
    large = jnp.minimum(large, nb - 1)
    return jnp.where(rel > 0, nb, 0) + jnp.where(n < max_exact, n, large)


def _a_bias_table(rel_bias, d):
    i = jnp.arange(A_SUB, dtype=jnp.int32)[:, None]
    j = jnp.arange(A_SUB + 2 * A_HALF, dtype=jnp.int32)[None, :]
    rel = j - A_HALF - i
    bucket = _t5_bucket(rel * d)
    hit = bucket[None, :, :, None] == jnp.arange(N_BUCKETS, dtype=jnp.int32)
    bias = jnp.sum(jnp.where(hit, rel_bias.astype(F32).T[:, None, None, :], 0.0), axis=-1)
    return jnp.where((jnp.abs(rel) <= A_HALF)[None], bias, NEG_INF)


def _rope_tables(seqs):
    pos = jnp.concatenate([jnp.arange(sl, dtype=jnp.int32) for _, sl in seqs])
    row = (pos // GRID_W).astype(F32)
    col = (pos % GRID_W).astype(F32)
    n_freq = HEAD_DIM // 4
    inv_freq = ROPE_THETA ** (-jnp.arange(n_freq, dtype=F32) / n_freq)
    ang_r = row[:, None] * inv_freq[None, :]
    ang_c = col[:, None] * inv_freq[None, :]
    ang = jnp.concatenate([ang_r, ang_r, ang_c, ang_c], axis=-1)
    sign = jnp.where((jnp.arange(HEAD_DIM) % (HEAD_DIM // 2)) < (HEAD_DIM // 4), -1.0, 1.0).astype(F32)
    return jnp.cos(ang), jnp.sin(ang) * sign[None, :]


def kernel(x_prompt, x_sample, w_in, b_gate, q_norm_g, k_norm_g, rel_bias, w_branch_a, w_branch_b, w_out,
           ln1_g, ln1_b, ln2_g, ln2_b, ffn_w_gate, ffn_w_up, ffn_w_down, router_w, exp_w_gate, exp_w_up,
           exp_w_down):
    seqs = []
    for xs in (x_prompt, x_sample):
        for _ in range(xs.shape[0]):
            seqs.append((sum(sl for _, sl in seqs), xs.shape[1]))
    seqs = tuple(seqs)
    T = sum(sl for _, sl in seqs)
    assert all(s0 % max(PROJ_TM, B_TQ) == 0 and sl % max(PROJ_TM, B_TQ) == 0 for s0, sl in seqs)

    x = jnp.concatenate([x_prompt.reshape(-1, D_MODEL), x_sample.reshape(-1, D_MODEL)], axis=0)
    xb = x.astype(BF16)
    cos, sin_signed = _rope_tables(seqs)
    biases = [_a_bias_table(rel_bias, d) for _, d in DILATED_PATTERNS]
    router_pad = jnp.pad(router_w.astype(F32), ((0, 0), (0, 0), (0, V7X_LANES - N_EXPERTS)))
    router_hi = router_pad.astype(BF16)
    router_lo = (router_pad - router_hi.astype(F32)).astype(BF16)
    router_split = jnp.concatenate([router_hi, router_lo], axis=-1)

    for l in range(DEPTH):
        a1, a4, a16, qb, kb, vb, gates = _in_proj(
            xb, w_in[l].astype(BF16), b_gate[l].reshape(1, 2 * D_MODEL),
            q_norm_g[l].reshape(1, HEAD_DIM), k_norm_g[l].reshape(1, HEAD_DIM), cos, sin_signed)
        o1, l1 = _attn_a(a1, biases[0], 1, seqs)
        o4, l4 = _attn_a(a4, biases[1], 4, seqs)
        o16, l16 = _attn_a(a16, biases[2], 16, seqs)
        ob = _attn_b(qb, kb, vb, seqs)
        moe = l % 2 == 1
        res = _branch(o1, o4, o16, l1, l4, l16, ob, gates, x,
                      w_branch_a[l].astype(BF16), w_branch_b[l].astype(BF16), w_out[l].astype(BF16),
                      ln1_g[l].reshape(1, D_MODEL), ln1_b[l].reshape(1, D_MODEL),
                      router_split[l // 2] if moe else None)
        g2, b2 = ln2_g[l].reshape(1, D_MODEL), ln2_b[l].reshape(1, D_MODEL)
        j = l // 2
        if not moe:
            x1, x1b = res
            x, xb = _ffn_dense(x1b, x1, ffn_w_gate[j].astype(BF16), ffn_w_up[j].astype(BF16),
                               ffn_w_down[j].astype(BF16), g2, b2)
        else:
            x1, x1c, route = res
            dest, src, tile_expert, tile_valid = _route_plan(route, MOE_TM)
            xs = _moe_gather(x1c, src)
            ys = _moe_ffn(xs, tile_expert, tile_valid, exp_w_gate[j].astype(BF16), exp_w_up[j].astype(BF16),
                          exp_w_down[j].astype(BF16))
            x, xb = _moe_combine(ys, dest, route, x1, g2, b2)

    n_prompt = x_prompt.shape[0] * x_prompt.shape[1]
    return (x[:n_prompt].reshape(x_prompt.shape), x[n_prompt:].reshape(x_sample.shape))
```

```python
import functools
import math

import numpy as np
import jax
import jax.numpy as jnp
from jax import lax
from jax.experimental import pallas as pl
from jax.experimental.pallas import tpu as pltpu

D_MODEL = 2048
DEPTH = 2
HEAD_DIM = 128
A_HEADS = 8
A_WIDTH = A_HEADS * HEAD_DIM
DILATED_PATTERNS = ((128, 1), (512, 4), (2048, 16))
A_HALF = 64
B_HEADS = 8
B_KV_HEADS = 2
B_GROUPS = B_HEADS // B_KV_HEADS
B_WIDTH = B_HEADS * HEAD_DIM
B_KV_WIDTH = B_KV_HEADS * HEAD_DIM
GRID_W = 64
ROPE_THETA = 10000.0
N_BUCKETS = 32
MAX_DISTANCE = 1024
IN_COLS = 3 * A_WIDTH + B_WIDTH + 2 * B_KV_WIDTH + 2 * D_MODEL
D_FF = 5632
N_EXPERTS = 8
TOP_K = 2
D_FF_EXPERT = 7168
DEEPNORM_ALPHA = (2.0 * DEPTH) ** 0.25
LN_EPS = 1e-5
RMS_EPS = 1e-6
NEG_INF = -1e30
ATTN_SCALE = HEAD_DIM ** -0.5

V7X_LANES = 128
V7X_VMEM_BYTES = 64 * 1024 * 1024
ROW_SLAB = D_MODEL // V7X_LANES

PROJ_TM = 1024
PROJ_TN = 512
A_BLOCK = 256
A_SUB = 128
B_TQ = 1024
B_TK = 4096
B_CK = 512
BRANCH_TM = 256
FFN_TM = 512
FFN_TF = 512
MOE_TM = 512
MOE_TF = 512
COMBINE_TM = 256

BF16 = jnp.bfloat16
F32 = jnp.float32

assert all(w // (2 * d) == A_HALF for w, d in DILATED_PATTERNS)
assert IN_COLS % PROJ_TN == 0 and D_FF % FFN_TF == 0 and D_FF_EXPERT % MOE_TF == 0


def _vmem_limit(nbytes):
    return int(min(nbytes, V7X_VMEM_BYTES - 4 * 1024 * 1024))


def _params(n_axes, vmem_bytes):
    return pltpu.CompilerParams(
        dimension_semantics=("arbitrary",) * n_axes, vmem_limit_bytes=_vmem_limit(vmem_bytes))


def _any_eq(b, values):
    return functools.reduce(jnp.logical_or, [b == v for v in values])


def _layer_norm_rows(z, g, b):
    mu = jnp.mean(z, axis=-1, keepdims=True)
    zc = z - mu
    var = jnp.mean(zc * zc, axis=-1, keepdims=True)
    return zc * lax.rsqrt(var + LN_EPS) * g + b


_N_QA, _N_KA, _N_VA = 0, A_WIDTH // PROJ_TN, 2 * A_WIDTH // PROJ_TN
_N_QB = 3 * A_WIDTH // PROJ_TN
_N_KVB = _N_QB + B_WIDTH // PROJ_TN
_N_GATE = _N_KVB + 1
_N_TILES = IN_COLS // PROJ_TN
assert 2 * B_KV_WIDTH == PROJ_TN and A_WIDTH % PROJ_TN == 0


def _rms_rope_store(acc, n_heads, g, cos, sin_signed, post_scale, out_ref):
    lane = lax.broadcasted_iota(jnp.int32, (acc.shape[0], HEAD_DIM), 1)
    low = (lane % (HEAD_DIM // 2)) < (HEAD_DIM // 4)
    for h in range(n_heads):
        xh = acc[:, h * HEAD_DIM:(h + 1) * HEAD_DIM]
        y = xh * lax.rsqrt(jnp.mean(xh * xh, axis=-1, keepdims=True) + RMS_EPS) * g
        rot = jnp.where(low, pltpu.roll(y, HEAD_DIM - HEAD_DIM // 4, 1), pltpu.roll(y, HEAD_DIM // 4, 1))
        roped = y * cos + rot * sin_signed
        if post_scale != 1.0:
            roped = roped * post_scale
        out_ref[:, h * HEAD_DIM:(h + 1) * HEAD_DIM] = roped.astype(out_ref.dtype)


B_Q_PRESCALE = ATTN_SCALE * math.log2(math.e)


def _in_proj_kernel(x_ref, w_ref, bg_ref, qg_ref, kg_ref, cos_ref, sin_ref,
                    a1_ref, a4_ref, a16_ref, qb_ref, kb_ref, vb_ref, gate_ref, acc_ref):
    n = pl.program_id(1)
    tm = x_ref.shape[0]

    def product():
        return jnp.dot(x_ref[...], w_ref[...], preferred_element_type=F32)

    @pl.when(n < _N_QB)
    def _():
        acc = product()
        a1_ref[0] = acc.astype(a1_ref.dtype)
        for c in range(acc_ref.shape[0]):
            cs = slice(c * V7X_LANES, (c + 1) * V7X_LANES)
            acc_ref[c] = acc[:, cs]
            for d, o in ((4, a4_ref), (16, a16_ref)):
                for r in range(d):
                    o[r, :, cs] = acc_ref[c, pl.ds(r, tm // d, stride=d), :].astype(o.dtype)

    @pl.when(jnp.logical_and(n >= _N_QB, n < _N_KVB))
    def _():
        _rms_rope_store(product(), PROJ_TN // HEAD_DIM, qg_ref[...], cos_ref[...], sin_ref[...], B_Q_PRESCALE,
                        qb_ref)

    @pl.when(n == _N_KVB)
    def _():
        acc = product()
        _rms_rope_store(acc[:, :B_KV_WIDTH], B_KV_HEADS, kg_ref[...], cos_ref[...], sin_ref[...], 1.0, kb_ref)
        vb_ref[...] = acc[:, B_KV_WIDTH:].astype(vb_ref.dtype)

    @pl.when(n >= _N_GATE)
    def _():
        gate_ref[...] = jax.nn.sigmoid(product() + bg_ref[...]).astype(gate_ref.dtype)


def _in_proj(xb, w, bg, qg, kg, cos, sin_signed):
    T = xb.shape[0]
    tm, tn = PROJ_TM, PROJ_TN
    grid = (T // tm, _N_TILES)
    a_tiles = A_WIDTH // tn

    def clampn(lo, cnt):
        return lambda m, n: (m, jnp.clip(n - lo, 0, cnt - 1))

    def dil_spec(d):
        def index(m, n):
            nc = jnp.minimum(n, 3 * a_tiles - 1)
            return (nc // a_tiles, 0, m, nc % a_tiles)
        return pl.BlockSpec((None, d, tm // d, tn), index)

    def dil_shape(d):
        return jax.ShapeDtypeStruct((3, d, T // d, A_WIDTH), BF16)

    in_specs = [
        pl.BlockSpec((tm, D_MODEL), lambda m, n: (m, 0)),
        pl.BlockSpec((D_MODEL, tn), lambda m, n: (0, n)),
        pl.BlockSpec((1, tn), lambda m, n: (0, jnp.clip(n - _N_GATE, 0, 2 * D_MODEL // tn - 1))),
        pl.BlockSpec((1, HEAD_DIM), lambda m, n: (0, 0)),
        pl.BlockSpec((1, HEAD_DIM), lambda m, n: (0, 0)),
        pl.BlockSpec((tm, HEAD_DIM), lambda m, n: (m, 0)),
        pl.BlockSpec((tm, HEAD_DIM), lambda m, n: (m, 0)),
    ]
    out_specs = [dil_spec(1), dil_spec(4), dil_spec(16),
                 pl.BlockSpec((tm, tn), clampn(_N_QB, B_WIDTH // tn)),
                 pl.BlockSpec((tm, B_KV_WIDTH), lambda m, n: (m, 0)),
                 pl.BlockSpec((tm, B_KV_WIDTH), lambda m, n: (m, 0)),
                 pl.BlockSpec((tm, tn), clampn(_N_GATE, 2 * D_MODEL // tn))]
    out_shape = [dil_shape(1), dil_shape(4), dil_shape(16),
                 jax.ShapeDtypeStruct((T, B_WIDTH), BF16),
                 jax.ShapeDtypeStruct((T, B_KV_WIDTH), BF16),
                 jax.ShapeDtypeStruct((T, B_KV_WIDTH), BF16),
                 jax.ShapeDtypeStruct((T, 2 * D_MODEL), BF16)]
    vmem = (2 * tm * D_MODEL * 2 + 2 * D_MODEL * tn * 2 + 7 * 2 * tm * tn * 2 + 4 * tm * HEAD_DIM * 4
            + 6 * tm * tn * 4 + (8 << 20))
    return pl.pallas_call(
        _in_proj_kernel, grid=grid, in_specs=in_specs, out_specs=out_specs, out_shape=out_shape,
        scratch_shapes=[pltpu.VMEM((tn // V7X_LANES, tm, V7X_LANES), F32)],
        compiler_params=_params(2, vmem), name="in_proj",
    )(xb, w, bg, qg, kg, cos, sin_signed)


def _attn_a_kernel(q_ref, kp_ref, kc_ref, kn_ref, vp_ref, vc_ref, vn_ref, bias_ref, o_ref, lse_ref,
                   *, first_blocks, last_blocks):
    b = pl.program_id(1)
    win = A_SUB + 2 * A_HALF
    is_first = _any_eq(b, first_blocks)
    is_last = _any_eq(b, last_blocks)
    col = lax.broadcasted_iota(jnp.int32, (A_SUB, win), 1)
    lane = lax.broadcasted_iota(jnp.int32, (A_SUB, V7X_LANES), 1)
    n_sub = A_BLOCK // A_SUB

    def window(prev_ref, cur_ref, next_ref, start, hs):
        parts = []
        if start < 0:
            parts.append(prev_ref[A_HALF + start:, hs])
        parts.append(cur_ref[max(start, 0):min(start + win, A_BLOCK), hs])
        if start + win > A_BLOCK:
            parts.append(next_ref[:start + win - A_BLOCK, hs])
        return parts[0] if len(parts) == 1 else jnp.concatenate(parts, axis=0)

    for sb in range(n_sub):
        start = sb * A_SUB - A_HALF
        rows = slice(sb * A_SUB, (sb + 1) * A_SUB)
        lo = jnp.where(is_first, -start, 0) if start < 0 else 0
        hi = jnp.where(is_last, A_BLOCK - start, win) if start + win > A_BLOCK else win
        outside = jnp.logical_or(col < lo, col >= hi)
        lse_tile = jnp.zeros((A_SUB, V7X_LANES), F32)
        for h in range(A_HEADS):
            hs = slice(h * HEAD_DIM, (h + 1) * HEAD_DIM)
            q = q_ref[rows, hs]
            k = window(kp_ref, kc_ref, kn_ref, start, hs)
            v = window(vp_ref, vc_ref, vn_ref, start, hs)
            s = lax.dot_general(q, k, (((1,), (1,)), ((), ())), preferred_element_type=F32)
            s = jnp.where(outside, NEG_INF, s * ATTN_SCALE + bias_ref[h])
            m = jnp.max(s, axis=1, keepdims=True)
            p = jnp.exp(s - m)
            l = jnp.sum(p, axis=1, keepdims=True)
            o = jnp.dot(p.astype(v.dtype), v, preferred_element_type=F32)
            o_ref[rows, hs] = o / l
            lse_tile = jnp.where(lane == h, m + jnp.log(l), lse_tile)
        lse_ref[rows, :] = lse_tile


def _attn_a(a, bias, d, seqs):
    _, _, L, _ = a.shape
    nb = L // A_BLOCK
    assert all(s0 % (d * A_BLOCK) == 0 and sl % (d * A_BLOCK) == 0 for s0, sl in seqs)
    first_blocks = tuple(s0 // d // A_BLOCK for s0, _ in seqs)
    last_blocks = tuple((s0 + sl) // d // A_BLOCK - 1 for s0, sl in seqs)

    def cur(which):
        return lambda r, b: (which, r, b, 0)

    per = A_BLOCK // A_HALF

    def prev(which):
        return lambda r, b: (which, r, jnp.where(_any_eq(b, first_blocks), b * per, b * per - 1), 0)

    def nxt(which):
        return lambda r, b: (which, r, jnp.where(_any_eq(b, last_blocks), b * per, (b + 1) * per), 0)

    blk = (None, None, A_BLOCK, A_WIDTH)
    halo = (None, None, A_HALF, A_WIDTH)
    win = A_SUB + 2 * A_HALF
    in_specs = [pl.BlockSpec(blk, cur(0)),
                pl.BlockSpec(halo, prev(1)), pl.BlockSpec(blk, cur(1)), pl.BlockSpec(halo, nxt(1)),
                pl.BlockSpec(halo, prev(2)), pl.BlockSpec(blk, cur(2)), pl.BlockSpec(halo, nxt(2)),
                pl.BlockSpec((A_HEADS, A_SUB, win), lambda r, b: (0, 0, 0))]
    out_idx = lambda r, b: (r, b, 0)
    out_specs = [pl.BlockSpec((None, A_BLOCK, A_WIDTH), out_idx), pl.BlockSpec((None, A_BLOCK, V7X_LANES), out_idx)]
    out_shape = [jax.ShapeDtypeStruct((d, L, A_WIDTH), F32), jax.ShapeDtypeStruct((d, L, V7X_LANES), F32)]
    vmem = 7 * 2 * A_BLOCK * A_WIDTH * 2 + 2 * A_HEADS * A_SUB * win * 4 + 2 * A_BLOCK * A_WIDTH * 4 + (8 << 20)
    return pl.pallas_call(
        functools.partial(_attn_a_kernel, first_blocks=first_blocks, last_blocks=last_blocks),
        grid=(d, nb), in_specs=in_specs, out_specs=out_specs, out_shape=out_shape,
        compiler_params=_params(2, vmem), name=f"attn_a_d{d}",
    )(a, a, a, a, a, a, a, bias)


def _attn_b_kernel(qi_ref, ki_ref, hi_ref, fl_ref, q_ref, k_ref, v_ref, o_ref, m_sc, l_sc, acc_sc):
    step = pl.program_id(0)
    flags = fl_ref[step]

    @pl.when((flags & 1) != 0)
    def _():
        m_sc[...] = jnp.full(m_sc.shape, NEG_INF, F32)
        l_sc[...] = jnp.zeros(l_sc.shape, F32)
        acc_sc[...] = jnp.zeros(acc_sc.shape, F32)


    tq = q_ref.shape[0]
    q = jnp.concatenate([q_ref[:, g * HEAD_DIM:(g + 1) * HEAD_DIM] for g in range(B_GROUPS)], axis=0)

    n_chunks = k_ref.shape[0] // B_CK

    def chunk_rows(ci):
        return pl.ds(pl.multiple_of(ci * B_CK, B_CK), B_CK)

    def scores(ci):
        return lax.dot_general(q, k_ref[chunk_rows(ci), :], (((1,), (1,)), ((), ())),
                               preferred_element_type=F32)

    def softmax_pv(ci, t):
        v = v_ref[chunk_rows(ci), :]
        m_prev = m_sc[...]
        m_new = jnp.maximum(m_prev, jnp.max(t, axis=1, keepdims=True))
        alpha = jnp.exp2(m_prev - m_new)
        p = jnp.exp2(t - jnp.concatenate([m_new] * (B_CK // V7X_LANES), axis=1))
        l_sc[...] = alpha * l_sc[...] + jnp.sum(p, axis=1, keepdims=True)
        acc_sc[...] = alpha * acc_sc[...] + jnp.dot(p.astype(v.dtype), v, preferred_element_type=F32)
        m_sc[...] = m_new

    def chunk(ci, carry):
        softmax_pv(ci, scores(ci))
        return carry

    lax.fori_loop(0, n_chunks, chunk, 0)

    @pl.when((flags & 2) != 0)
    def _():
        for g in range(B_GROUPS):
            rows = slice(g * tq, (g + 1) * tq)
            o_ref[:, g * HEAD_DIM:(g + 1) * HEAD_DIM] = (acc_sc[rows, :] / l_sc[rows, :]).astype(o_ref.dtype)


def _attn_b_schedule(seqs, tk):
    qi, ki, hi, fl = [], [], [], []
    for s0, sl in seqs:
        nkv = sl // tk
        for h in range(B_KV_HEADS):
            for qb in range(sl // B_TQ):
                for kb in range(nkv):
                    qi.append(s0 // B_TQ + qb)
                    ki.append(s0 // tk + kb)
                    hi.append(h)
                    fl.append((1 if kb == 0 else 0) | (2 if kb == nkv - 1 else 0))
    return [np.asarray(a, np.int32) for a in (qi, ki, hi, fl)]


def _attn_b(qb, kb, vb, seqs):
    T = qb.shape[0]
    tk = min([B_TK] + [sl for _, sl in seqs])
    assert all(sl % tk == 0 and s0 % tk == 0 and sl % B_TQ == 0 for s0, sl in seqs)
    qi, ki, hi, fl = _attn_b_schedule(seqs, tk)
    gw = B_GROUPS * HEAD_DIM
    rows = B_GROUPS * B_TQ
    grid_spec = pltpu.PrefetchScalarGridSpec(
        num_scalar_prefetch=4, grid=(len(qi),),
        in_specs=[pl.BlockSpec((B_TQ, gw), lambda s, qi, ki, hi, fl: (qi[s], hi[s])),
                  pl.BlockSpec((tk, HEAD_DIM), lambda s, qi, ki, hi, fl: (ki[s], hi[s])),
                  pl.BlockSpec((tk, HEAD_DIM), lambda s, qi, ki, hi, fl: (ki[s], hi[s]))],
        out_specs=pl.BlockSpec((B_TQ, gw), lambda s, qi, ki, hi, fl: (qi[s], hi[s])),
        scratch_shapes=[pltpu.VMEM((rows, V7X_LANES), F32)] * 3)
    vmem = 4 * B_TQ * gw * 2 + 4 * tk * HEAD_DIM * 2 + 3 * rows * V7X_LANES * 4 + 4 * rows * B_CK * 4 + (8 << 20)
    return pl.pallas_call(
        _attn_b_kernel, grid_spec=grid_spec, out_shape=jax.ShapeDtypeStruct((T, B_WIDTH), BF16),
        compiler_params=_params(1, vmem), name="attn_b",
    )(jnp.asarray(qi), jnp.asarray(ki), jnp.asarray(hi), jnp.asarray(fl), qb, kb, vb)


def _branch_kernel(*refs, with_router):
    (o1_ref, o4_ref, o16_ref, l1_ref, l4_ref, l16_ref, ob_ref, sga_ref, sgb_ref, x_ref,
     wba_ref, wbb_ref, wo_ref, g_ref, b_ref) = refs[:15]
    rest = refs[15:]
    if with_router:
        rw_ref, x1_ref, x1c_ref, route_ref, o4_sc, o16_sc, l4_sc, l16_sc = rest
    else:
        x1_ref, x1b_ref, o4_sc, o16_sc, l4_sc, l16_sc = rest
    tm = x_ref.shape[0]
    for d, src, dst, lsrc, ldst in ((4, o4_ref, o4_sc, l4_ref, l4_sc), (16, o16_ref, o16_sc, l16_ref, l16_sc)):
        for r in range(d):
            ldst[pl.ds(r, tm // d, stride=d), :] = lsrc[r]
            for h in range(A_HEADS):
                dst[h, pl.ds(r, tm // d, stride=d), :] = src[r, :, h * HEAD_DIM:(h + 1) * HEAD_DIM]
    l1, l4, l16 = l1_ref[0], l4_sc[...], l16_sc[...]
    mx = jnp.maximum(jnp.maximum(l1, l4), l16)
    e1, e4, e16 = jnp.exp(l1 - mx), jnp.exp(l4 - mx), jnp.exp(l16 - mx)
    den = e1 + e4 + e16
    w1, w4, w16 = e1 / den, e4 / den, e16 / den
    parts = []
    for h in range(A_HEADS):
        hs = slice(h * HEAD_DIM, (h + 1) * HEAD_DIM)
        oa_h = w1[:, h:h + 1] * o1_ref[0, :, hs] + w4[:, h:h + 1] * o4_sc[h] + w16[:, h:h + 1] * o16_sc[h]
        parts.append(oa_h.astype(BF16))
    oa = jnp.concatenate(parts, axis=1)
    ya = jnp.dot(oa, wba_ref[...], preferred_element_type=F32)
    yb = jnp.dot(ob_ref[...], wbb_ref[...], preferred_element_type=F32)
    merged = sga_ref[...].astype(F32) * ya + sgb_ref[...].astype(F32) * yb
    y = jnp.dot(merged.astype(BF16), wo_ref[...], preferred_element_type=F32)
    out = _layer_norm_rows(DEEPNORM_ALPHA * x_ref[...] + y, g_ref[...], b_ref[...])
    x1_ref[...] = out
    if not with_router:
        x1b_ref[...] = out.astype(BF16)
    if with_router:
        for c in range(ROW_SLAB):
            x1c_ref[pl.ds(c, tm, stride=ROW_SLAB), :] = out[:, c * V7X_LANES:(c + 1) * V7X_LANES]
        out_hi = out.astype(BF16)
        out_lo = (out - out_hi.astype(F32)).astype(BF16)
        both = jnp.dot(out_hi, rw_ref[...], preferred_element_type=F32)
        logits = (both[:, :V7X_LANES] + both[:, V7X_LANES:]
                  + jnp.dot(out_lo, rw_ref[:, :V7X_LANES], preferred_element_type=F32))
        lane = lax.broadcasted_iota(jnp.int32, logits.shape, 1)
        logits = jnp.where(lane < N_EXPERTS, logits, -jnp.inf)
        v1 = jnp.max(logits, axis=1, keepdims=True)
        i1 = jnp.min(jnp.where(logits == v1, lane, V7X_LANES), axis=1, keepdims=True)
        rem = jnp.where(lane == i1, -jnp.inf, logits)
        v2 = jnp.max(rem, axis=1, keepdims=True)
        i2 = jnp.min(jnp.where(rem == v2, lane, V7X_LANES), axis=1, keepdims=True)
        e2 = jnp.exp(v2 - v1)
        g1 = 1.0 / (1.0 + e2)
        g2 = e2 / (1.0 + e2)
        route = jnp.where(lane == 0, i1.astype(F32),
                          jnp.where(lane == 1, i2.astype(F32),
                                    jnp.where(lane == 2, g1, jnp.where(lane == 3, g2, 0.0))))
        route_ref[...] = route


def _branch(o1, o4, o16, l1, l4, l16, ob, gates, x, wba, wbb, wo, g, b, router_w=None):
    T = x.shape[0]
    tm = BRANCH_TM
    with_router = router_w is not None
    row = lambda m: (m, 0)
    const = lambda m: (0, 0)
    single = pl.Buffered(1)
    in_specs = [
        pl.BlockSpec((1, tm, A_WIDTH), lambda m: (0, m, 0)),
        pl.BlockSpec((4, tm // 4, A_WIDTH), lambda m: (0, m, 0)),
        pl.BlockSpec((16, tm // 16, A_WIDTH), lambda m: (0, m, 0)),
        pl.BlockSpec((1, tm, V7X_LANES), lambda m: (0, m, 0)),
        pl.BlockSpec((4, tm // 4, V7X_LANES), lambda m: (0, m, 0)),
        pl.BlockSpec((16, tm // 16, V7X_LANES), lambda m: (0, m, 0)),
        pl.BlockSpec((tm, B_WIDTH), row),
        pl.BlockSpec((tm, D_MODEL), lambda m: (m, 0)),
        pl.BlockSpec((tm, D_MODEL), lambda m: (m, 1)),
        pl.BlockSpec((tm, D_MODEL), row),
        pl.BlockSpec((A_WIDTH, D_MODEL), const, pipeline_mode=single),
        pl.BlockSpec((B_WIDTH, D_MODEL), const, pipeline_mode=single),
        pl.BlockSpec((D_MODEL, D_MODEL), const, pipeline_mode=single),
        pl.BlockSpec((1, D_MODEL), const),
        pl.BlockSpec((1, D_MODEL), const),
    ]
    args = [o1, o4, o16, l1, l4, l16, ob, gates, gates, x, wba, wbb, wo, g, b]
    out_specs = [pl.BlockSpec((tm, D_MODEL), row)]
    out_shape = [jax.ShapeDtypeStruct((T, D_MODEL), F32)]
    if with_router:
        in_specs.append(pl.BlockSpec((D_MODEL, 2 * V7X_LANES), const, pipeline_mode=single))
        args.append(router_w)
        out_specs += [pl.BlockSpec((tm * ROW_SLAB, V7X_LANES), row), pl.BlockSpec((tm, V7X_LANES), row)]
        out_shape += [jax.ShapeDtypeStruct((T * ROW_SLAB, V7X_LANES), F32),
                      jax.ShapeDtypeStruct((T, V7X_LANES), F32)]
    else:
        out_specs.append(pl.BlockSpec((tm, D_MODEL), row))
        out_shape.append(jax.ShapeDtypeStruct((T, D_MODEL), BF16))
    scratch = [pltpu.VMEM((A_HEADS, tm, HEAD_DIM), F32), pltpu.VMEM((A_HEADS, tm, HEAD_DIM), F32),
               pltpu.VMEM((tm, V7X_LANES), F32), pltpu.VMEM((tm, V7X_LANES), F32)]
    vmem = ((A_WIDTH + B_WIDTH + D_MODEL) * D_MODEL * 2 + D_MODEL * V7X_LANES * 4
            + 2 * 3 * tm * A_WIDTH * 4 + 2 * tm * A_WIDTH * 4 + 2 * tm * B_WIDTH * 2 + 4 * tm * D_MODEL * 2
            + 2 * tm * D_MODEL * 4 + 2 * tm * D_MODEL * 6 + 8 * tm * D_MODEL * 4 + (8 << 20))
    return pl.pallas_call(
        functools.partial(_branch_kernel, with_router=with_router),
        grid=(T // tm,), in_specs=in_specs, out_specs=out_specs, out_shape=out_shape,
        scratch_shapes=scratch, compiler_params=_params(1, vmem),
        name="branch_router" if with_router else "branch",
    )(*args)


def _swiglu_accumulate(xb, wg_ref, wu_ref, wd_ref, acc_ref):
    g = jnp.dot(xb, wg_ref[...], preferred_element_type=F32)
    u = jnp.dot(xb, wu_ref[...], preferred_element_type=F32)
    a = (g * jax.nn.sigmoid(g) * u).astype(BF16)
    acc_ref[...] += jnp.dot(a, wd_ref[...], preferred_element_type=F32)


def _ffn_dense_kernel(xb_ref, x_ref, wg_ref, wu_ref, wd_ref, g_ref, b_ref, o_ref, ob_ref):
    j = pl.program_id(1)

    @pl.when(j == 0)
    def _():
        o_ref[...] = jnp.zeros(o_ref.shape, F32)

    _swiglu_accumulate(xb_ref[...], wg_ref, wu_ref, wd_ref, o_ref)

    @pl.when(j == pl.num_programs(1) - 1)
    def _():
        out = _layer_norm_rows(DEEPNORM_ALPHA * x_ref[...] + o_ref[...], g_ref[...], b_ref[...])
        o_ref[...] = out
        ob_ref[...] = out.astype(BF16)


def _ffn_dense(xb, x, wg, wu, wd, g, b):
    T = x.shape[0]
    tm, tf = FFN_TM, FFN_TF
    row = lambda m, j: (m, 0)
    const = lambda m, j: (0, 0)
    in_specs = [pl.BlockSpec((tm, D_MODEL), row), pl.BlockSpec((tm, D_MODEL), row),
                pl.BlockSpec((D_MODEL, tf), lambda m, j: (0, j)),
                pl.BlockSpec((D_MODEL, tf), lambda m, j: (0, j)),
                pl.BlockSpec((tf, D_MODEL), lambda m, j: (j, 0)),
                pl.BlockSpec((1, D_MODEL), const), pl.BlockSpec((1, D_MODEL), const)]
    out_specs = [pl.BlockSpec((tm, D_MODEL), row), pl.BlockSpec((tm, D_MODEL), row)]
    out_shape = [jax.ShapeDtypeStruct((T, D_MODEL), F32), jax.ShapeDtypeStruct((T, D_MODEL), BF16)]
    vmem = (2 * tm * D_MODEL * 2 + 2 * tm * D_MODEL * 4 + 2 * 3 * D_MODEL * tf * 2 + 2 * tm * D_MODEL * 6
            + 4 * tm * tf * 4 + 2 * tm * D_MODEL * 4 + (8 << 20))
    return pl.pallas_call(
        _ffn_dense_kernel, grid=(T // tm, D_FF // tf), in_specs=in_specs, out_specs=out_specs,
        out_shape=out_shape, compiler_params=_params(2, vmem), name="ffn_dense",
    )(xb, x, wg, wu, wd, g, b)


def _moe_gather_kernel(src_ref, x_hbm, o_ref, slab_sc, sem):
    i = pl.program_id(0)
    n_tiles = pl.num_programs(0) - 1
    tm = src_ref.shape[2]
    slot = lax.rem(i, 2)

    def slab_copy(s, j, t):
        return pltpu.make_async_copy(x_hbm.at[pl.ds(t * ROW_SLAB, ROW_SLAB), :],
                                     slab_sc.at[s, pl.ds(j * ROW_SLAB, ROW_SLAB), :], sem.at[s])

    @pl.when(i < n_tiles)
    def _():
        def issue(j, c):
            slab_copy(slot, j, src_ref[0, 0, j]).start()
            return c
        lax.fori_loop(0, tm, issue, 0, unroll=8)

    @pl.when(i > 0)
    def _():
        done = 1 - slot

        def wait(j, c):
            slab_copy(done, j, 0).wait()
            return c
        lax.fori_loop(0, tm, wait, 0, unroll=8)
        for c in range(ROW_SLAB):
            cs = slice(c * V7X_LANES, (c + 1) * V7X_LANES)
            o_ref[:, cs] = slab_sc[done, pl.ds(c, tm, stride=ROW_SLAB), :].astype(o_ref.dtype)


def _moe_gather(xc, src):
    n_tiles, _, tm = src.shape
    vmem = 2 * tm * D_MODEL * 4 + 2 * tm * D_MODEL * 2 + 2 * tm * D_MODEL * 4 + (8 << 20)
    return pl.pallas_call(
        _moe_gather_kernel, grid=(n_tiles + 1,),
        in_specs=[pl.BlockSpec((1, 1, tm), lambda i: (jnp.minimum(i, n_tiles - 1), 0, 0), memory_space=pltpu.SMEM),
                  pl.BlockSpec(memory_space=pl.ANY)],
        out_specs=pl.BlockSpec((tm, D_MODEL), lambda i: (jnp.maximum(i - 1, 0), 0)),
        out_shape=jax.ShapeDtypeStruct((n_tiles * tm, D_MODEL), BF16),
        scratch_shapes=[pltpu.VMEM((2, tm * ROW_SLAB, V7X_LANES), F32), pltpu.SemaphoreType.DMA((2,))],
        compiler_params=_params(1, vmem), name="moe_gather",
    )(src, xc)


def _moe_ffn_kernel(te_ref, tv_ref, x_ref, wg_ref, wu_ref, wd_ref, o_ref):
    i = pl.program_id(0)
    j = pl.program_id(1)

    @pl.when(j == 0)
    def _():
        o_ref[...] = jnp.zeros(o_ref.shape, F32)

    @pl.when(tv_ref[i] != 0)
    def _():
        _swiglu_accumulate(x_ref[...], wg_ref, wu_ref, wd_ref, o_ref)


def _moe_ffn(xs, tile_expert, tile_valid, wg, wu, wd):
    P = xs.shape[0]
    tm, tf = MOE_TM, MOE_TF
    nj = D_FF_EXPERT // tf

    def jeff(i, j, tv):
        return jnp.where(tv[i] != 0, j, nj - 1)

    grid_spec = pltpu.PrefetchScalarGridSpec(
        num_scalar_prefetch=2, grid=(P // tm, nj),
        in_specs=[pl.BlockSpec((tm, D_MODEL), lambda i, j, te, tv: (i, 0)),
                  pl.BlockSpec((None, D_MODEL, tf), lambda i, j, te, tv: (te[i], 0, jeff(i, j, tv))),
                  pl.BlockSpec((None, D_MODEL, tf), lambda i, j, te, tv: (te[i], 0, jeff(i, j, tv))),
                  pl.BlockSpec((None, tf, D_MODEL), lambda i, j, te, tv: (te[i], jeff(i, j, tv), 0))],
        out_specs=pl.BlockSpec((tm, D_MODEL), lambda i, j, te, tv: (i, 0)))
    vmem = (2 * tm * D_MODEL * 2 + 2 * 3 * D_MODEL * tf * 2 + 2 * tm * D_MODEL * 4
            + 4 * tm * tf * 4 + 2 * tm * D_MODEL * 4 + (8 << 20))
    return pl.pallas_call(
        _moe_ffn_kernel, grid_spec=grid_spec, out_shape=jax.ShapeDtypeStruct((P, D_MODEL), F32),
        compiler_params=_params(2, vmem), name="moe_ffn",
    )(tile_expert, tile_valid, xs, wg, wu, wd)


def _moe_combine_kernel(dest_ref, y_hbm, route_ref, x_ref, g_ref, b_ref, o_ref, ob_ref, y0_sc, y1_sc, sem):
    tm = x_ref.shape[0]

    def row_copy(j, k, p):
        dst = y0_sc if k == 0 else y1_sc
        return pltpu.make_async_copy(y_hbm.at[pl.ds(p, 1), :], dst.at[pl.ds(j, 1), :], sem)

    def issue(j, c):
        row_copy(j, 0, dest_ref[0, 0, 2 * j]).start()
        row_copy(j, 1, dest_ref[0, 0, 2 * j + 1]).start()
        return c

    def wait(j, c):
        row_copy(j, 0, 0).wait()
        row_copy(j, 1, 0).wait()
        return c

    lax.fori_loop(0, tm, issue, 0, unroll=8)
    lax.fori_loop(0, tm, wait, 0, unroll=8)
    route = route_ref[...]
    f = route[:, 2:3] * y0_sc[...] + route[:, 3:4] * y1_sc[...]
    out = _layer_norm_rows(DEEPNORM_ALPHA * x_ref[...] + f, g_ref[...], b_ref[...])
    o_ref[...] = out
    ob_ref[...] = out.astype(BF16)


def _moe_combine(y, dest, route, x, g, b):
    T = x.shape[0]
    tm = COMBINE_TM
    row = lambda m: (m, 0)
    const = lambda m: (0, 0)
    vmem = 2 * tm * D_MODEL * 4 + 2 * tm * D_MODEL * 4 + 2 * tm * D_MODEL * 6 + 6 * tm * D_MODEL * 4 + (8 << 20)
    return pl.pallas_call(
        _moe_combine_kernel, grid=(T // tm,),
        in_specs=[pl.BlockSpec((1, 1, 2 * tm), lambda m: (m, 0, 0), memory_space=pltpu.SMEM),
                  pl.BlockSpec(memory_space=pl.ANY),
                  pl.BlockSpec((tm, V7X_LANES), row),
                  pl.BlockSpec((tm, D_MODEL), row),
                  pl.BlockSpec((1, D_MODEL), const), pl.BlockSpec((1, D_MODEL), const)],
        out_specs=[pl.BlockSpec((tm, D_MODEL), row), pl.BlockSpec((tm, D_MODEL), row)],
        out_shape=[jax.ShapeDtypeStruct((T, D_MODEL), F32), jax.ShapeDtypeStruct((T, D_MODEL), BF16)],
        scratch_shapes=[pltpu.VMEM((tm, D_MODEL), F32), pltpu.VMEM((tm, D_MODEL), F32),
                        pltpu.SemaphoreType.DMA(())],
        compiler_params=_params(1, vmem), name="moe_combine",
    )(dest.reshape(T // tm, 1, 2 * tm), y, route, x, g, b)


def _cumsum_rows(x):
    n, e = x.shape
    blocks = x.reshape(n // V7X_LANES, V7X_LANES, e).astype(F32)
    tri = (jnp.arange(V7X_LANES)[:, None] >= jnp.arange(V7X_LANES)[None, :]).astype(F32)
    within = jnp.einsum("ij,bje->bie", tri, blocks)
    totals = within[:, -1, :]
    offsets = jnp.cumsum(totals, axis=0) - totals
    return (within + offsets[:, None, :]).reshape(n, e).astype(jnp.int32)


def _route_plan(route, tm):
    T = route.shape[0]
    n_assign = T * TOP_K
    n_tiles = n_assign // tm + N_EXPERTS
    flat_e = route[:, :TOP_K].astype(jnp.int32).reshape(n_assign)
    onehot = (flat_e[:, None] == jnp.arange(N_EXPERTS, dtype=jnp.int32)[None, :]).astype(jnp.int32)
    csum = _cumsum_rows(onehot)
    rank = jnp.sum(csum * onehot, axis=1) - 1
    counts = csum[-1]
    tiles_per = (counts + tm - 1) // tm
    tile_end = jnp.cumsum(tiles_per)
    row_start = (tile_end - tiles_per) * tm
    dest = jnp.sum(onehot * row_start[None, :], axis=1) + rank
    tile_ids = jnp.arange(n_tiles, dtype=jnp.int32)
    n_valid = tile_end[-1]
    tile_valid = (tile_ids < n_valid).astype(jnp.int32)
    owner = jnp.sum((jnp.minimum(tile_ids, n_valid - 1)[:, None] >= tile_end[None, :]).astype(jnp.int32), axis=1)
    tile_expert = jnp.minimum(owner, N_EXPERTS - 1).astype(jnp.int32)
    src = jnp.zeros((n_tiles * tm,), jnp.int32).at[dest].set(jnp.arange(n_assign, dtype=jnp.int32) // TOP_K)
    return dest.reshape(T, TOP_K), src.reshape(n_tiles, 1, tm), tile_expert, tile_valid


def _t5_bucket(rel):
    nb = N_BUCKETS // 2
    max_exact = nb // 2
    n = jnp.abs(rel)
    large = max_exact + (jnp.log(jnp.maximum(n, 1).astype(F32) / max_exact)
                         / math.log(MAX_DISTANCE / max_exact) * (nb - max_exact)).astype(jnp.int32)
    large = jnp.minimum(large, nb - 1)
    return jnp.where(rel > 0, nb, 0) + jnp.where(n < max_exact, n, large)


def _a_bias_table(rel_bias, d):
    i = jnp.arange(A_SUB, dtype=jnp.int32)[:, None]
    j = jnp.arange(A_SUB + 2 * A_HALF, dtype=jnp.int32)[None, :]
    rel = j - A_HALF - i
    bucket = _t5_bucket(rel * d)
    hit = bucket[None, :, :, None] == jnp.arange(N_BUCKETS, dtype=jnp.int32)
    bias = jnp.sum(jnp.where(hit, rel_bias.astype(F32).T[:, None, None, :], 0.0), axis=-1)
    return jnp.where((jnp.abs(rel) <= A_HALF)[None], bias, NEG_INF)


def _rope_tables(seqs):
    pos = jnp.concatenate([jnp.arange(sl, dtype=jnp.int32) for _, sl in seqs])
    row = (pos // GRID_W).astype(F32)
    col = (pos % GRID_W).astype(F32)
    n_freq = HEAD_DIM // 4
    inv_freq = ROPE_THETA ** (-jnp.arange(n_freq, dtype=F32) / n_freq)
    ang_r = row[:, None] * inv_freq[None, :]
    ang_c = col[:, None] * inv_freq[None, :]
    ang = jnp.concatenate([ang_r, ang_r, ang_c, ang_c], axis=-1)
    sign = jnp.where((jnp.arange(HEAD_DIM) % (HEAD_DIM // 2)) < (HEAD_DIM // 4), -1.0, 1.0).astype(F32)
    return jnp.cos(ang), jnp.sin(ang) * sign[None, :]


def kernel(x_prompt, x_sample, w_in, b_gate, q_norm_g, k_norm_g, rel_bias, w_branch_a, w_branch_b, w_out,
           ln1_g, ln1_b, ln2_g, ln2_b, ffn_w_gate, ffn_w_up, ffn_w_down, router_w, exp_w_gate, exp_w_up,
           exp_w_down):
    seqs = []
    for xs in (x_prompt, x_sample):
        for _ in range(xs.shape[0]):
            seqs.append((sum(sl for _, sl in seqs), xs.shape[1]))
    seqs = tuple(seqs)
    T = sum(sl for _, sl in seqs)
    assert all(s0 % max(PROJ_TM, B_TQ) == 0 and sl % max(PROJ_TM, B_TQ) == 0 for s0, sl in seqs)

    x = jnp.concatenate([x_prompt.reshape(-1, D_MODEL), x_sample.reshape(-1, D_MODEL)], axis=0)
    xb = x.astype(BF16)
    cos, sin_signed = _rope_tables(seqs)
    biases = [_a_bias_table(rel_bias, d) for _, d in DILATED_PATTERNS]
    router_pad = jnp.pad(router_w.astype(F32), ((0, 0), (0, 0), (0, V7X_LANES - N_EXPERTS)))
    router_hi = router_pad.astype(BF16)
    router_lo = (router_pad - router_hi.astype(F32)).astype(BF16)
    router_split = jnp.concatenate([router_hi, router_lo], axis=-1)

    for l in range(DEPTH):
        a1, a4, a16, qb, kb, vb, gates = _in_proj(
            xb, w_in[l].astype(BF16), b_gate[l].reshape(1, 2 * D_MODEL),
            q_norm_g[l].reshape(1, HEAD_DIM), k_norm_g[l].reshape(1, HEAD_DIM), cos, sin_signed)
        o1, l1 = _attn_a(a1, biases[0], 1, seqs)
        o4, l4 = _attn_a(a4, biases[1], 4, seqs)
        o16, l16 = _attn_a(a16, biases[2], 16, seqs)
        ob = _attn_b(qb, kb, vb, seqs)
        moe = l % 2 == 1
        res = _branch(o1, o4, o16, l1, l4, l16, ob, gates, x,
                      w_branch_a[l].astype(BF16), w_branch_b[l].astype(BF16), w_out[l].astype(BF16),
                      ln1_g[l].reshape(1, D_MODEL), ln1_b[l].reshape(1, D_MODEL),
                      router_split[l // 2] if moe else None)
        g2, b2 = ln2_g[l].reshape(1, D_MODEL), ln2_b[l].reshape(1, D_MODEL)
        j = l // 2
        if not moe:
            x1, x1b = res
            x, xb = _ffn_dense(x1b, x1, ffn_w_gate[j].astype(BF16), ffn_w_up[j].astype(BF16),
                               ffn_w_down[j].astype(BF16), g2, b2)
        else:
            x1, x1c, route = res
            dest, src, tile_expert, tile_valid = _route_plan(route, MOE_TM)
            xs = _moe_gather(x1c, src)
            ys = _moe_ffn(xs, tile_expert, tile_valid, exp_w_gate[j].astype(BF16), exp_w_up[j].astype(BF16),
                          exp_w_down[j].astype(BF16))
            x, xb = _moe_combine(ys, dest, route, x1, g2, b2)

    n_prompt = x_prompt.shape[0] * x_prompt.shape[1]
    return (x[:n_prompt].reshape(x_prompt.shape), x[n_prompt:].reshape(x_sample.shape))
```

```python
import functools
import math

import numpy as np
import jax
import jax.numpy as jnp
from jax import lax
from jax.experimental import pallas as pl
from jax.experimental.pallas import tpu as pltpu

D_MODEL = 2048
DEPTH = 2
HEAD_DIM = 128
A_HEADS = 8
A_WIDTH = A_HEADS * HEAD_DIM
DILATED_PATTERNS = ((128, 1), (512, 4), (2048, 16))
A_HALF = 64
B_HEADS = 8
B_KV_HEADS = 2
B_GROUPS = B_HEADS // B_KV_HEADS
B_WIDTH = B_HEADS * HEAD_DIM
B_KV_WIDTH = B_KV_HEADS * HEAD_DIM
GRID_W = 64
ROPE_THETA = 10000.0
N_BUCKETS = 32
MAX_DISTANCE = 1024
IN_COLS = 3 * A_WIDTH + B_WIDTH + 2 * B_KV_WIDTH + 2 * D_MODEL
D_FF = 5632
N_EXPERTS = 8
TOP_K = 2
D_FF_EXPERT = 7168
DEEPNORM_ALPHA = (2.0 * DEPTH) ** 0.25
LN_EPS = 1e-5
RMS_EPS = 1e-6
NEG_INF = -1e30
ATTN_SCALE = HEAD_DIM ** -0.5

V7X_LANES = 128
V7X_VMEM_BYTES = 64 * 1024 * 1024
ROW_SLAB = D_MODEL // V7X_LANES

PROJ_TM = 1024
PROJ_TN = 512
A_BLOCK = 256
A_SUB = 128
B_TQ = 1024
B_TK = 4096
B_CK = 512
BRANCH_TM = 256
FFN_TM = 512
FFN_TF = 512
MOE_TM = 512
MOE_TF = 512
COMBINE_TM = 256
DMA_UNROLL = 8

BF16 = jnp.bfloat16
F32 = jnp.float32

assert all(w // (2 * d) == A_HALF for w, d in DILATED_PATTERNS)
assert IN_COLS % PROJ_TN == 0 and D_FF % FFN_TF == 0 and D_FF_EXPERT % MOE_TF == 0


def _vmem_limit(nbytes):
    return int(min(nbytes, V7X_VMEM_BYTES - 4 * 1024 * 1024))


def _params(n_axes, vmem_bytes):
    return pltpu.CompilerParams(
        dimension_semantics=("arbitrary",) * n_axes, vmem_limit_bytes=_vmem_limit(vmem_bytes))


def _any_eq(b, values):
    return functools.reduce(jnp.logical_or, [b == v for v in values])


def _layer_norm_rows(z, g, b):
    mu = jnp.mean(z, axis=-1, keepdims=True)
    zc = z - mu
    var = jnp.mean(zc * zc, axis=-1, keepdims=True)
    return zc * lax.rsqrt(var + LN_EPS) * g + b


_N_QA, _N_KA, _N_VA = 0, A_WIDTH // PROJ_TN, 2 * A_WIDTH // PROJ_TN
_N_QB = 3 * A_WIDTH // PROJ_TN
_N_KVB = _N_QB + B_WIDTH // PROJ_TN
_N_GATE = _N_KVB + 1
_N_TILES = IN_COLS // PROJ_TN
assert 2 * B_KV_WIDTH == PROJ_TN and A_WIDTH % PROJ_TN == 0


def _rms_rope_store(acc, n_heads, g, cos, sin_signed, post_scale, out_ref):
    lane = lax.broadcasted_iota(jnp.int32, (acc.shape[0], HEAD_DIM), 1)
    low = (lane % (HEAD_DIM // 2)) < (HEAD_DIM // 4)
    for h in range(n_heads):
        xh = acc[:, h * HEAD_DIM:(h + 1) * HEAD_DIM]
        y = xh * lax.rsqrt(jnp.mean(xh * xh, axis=-1, keepdims=True) + RMS_EPS) * g
        rot = jnp.where(low, pltpu.roll(y, HEAD_DIM - HEAD_DIM // 4, 1), pltpu.roll(y, HEAD_DIM // 4, 1))
        roped = y * cos + rot * sin_signed
        if post_scale != 1.0:
            roped = roped * post_scale
        out_ref[:, h * HEAD_DIM:(h + 1) * HEAD_DIM] = roped.astype(out_ref.dtype)


B_Q_PRESCALE = ATTN_SCALE * math.log2(math.e)


def _in_proj_kernel(x_ref, w_ref, bg_ref, qg_ref, kg_ref, cos_ref, sin_ref,
                    a1_ref, a4_ref, a16_ref, qb_ref, kb_ref, vb_ref, gate_ref, acc_ref):
    n = pl.program_id(1)
    tm = x_ref.shape[0]

    def product():
        return jnp.dot(x_ref[...], w_ref[...], preferred_element_type=F32)

    @pl.when(n < _N_QB)
    def _():
        acc = product()
        a1_ref[0] = acc.astype(a1_ref.dtype)
        for c in range(acc_ref.shape[0]):
            cs = slice(c * V7X_LANES, (c + 1) * V7X_LANES)
            acc_ref[c] = acc[:, cs]
            for d, o in ((4, a4_ref), (16, a16_ref)):
                for r in range(d):
                    o[r, :, cs] = acc_ref[c, pl.ds(r, tm // d, stride=d), :].astype(o.dtype)

    @pl.when(jnp.logical_and(n >= _N_QB, n < _N_KVB))
    def _():
        _rms_rope_store(product(), PROJ_TN // HEAD_DIM, qg_ref[...], cos_ref[...], sin_ref[...], B_Q_PRESCALE,
                        qb_ref)

    @pl.when(n == _N_KVB)
    def _():
        acc = product()
        _rms_rope_store(acc[:, :B_KV_WIDTH], B_KV_HEADS, kg_ref[...], cos_ref[...], sin_ref[...], 1.0, kb_ref)
        vb_ref[...] = acc[:, B_KV_WIDTH:].astype(vb_ref.dtype)

    @pl.when(n >= _N_GATE)
    def _():
        gate_ref[...] = jax.nn.sigmoid(product() + bg_ref[...]).astype(gate_ref.dtype)


def _in_proj(xb, w, bg, qg, kg, cos, sin_signed):
    T = xb.shape[0]
    tm, tn = PROJ_TM, PROJ_TN
    grid = (T // tm, _N_TILES)
    a_tiles = A_WIDTH // tn

    def clampn(lo, cnt):
        return lambda m, n: (m, jnp.clip(n - lo, 0, cnt - 1))

    def dil_spec(d):
        def index(m, n):
            nc = jnp.minimum(n, 3 * a_tiles - 1)
            return (nc // a_tiles, 0, m, nc % a_tiles)
        return pl.BlockSpec((None, d, tm // d, tn), index)

    def dil_shape(d):
        return jax.ShapeDtypeStruct((3, d, T // d, A_WIDTH), BF16)

    in_specs = [
        pl.BlockSpec((tm, D_MODEL), lambda m, n: (m, 0)),
        pl.BlockSpec((D_MODEL, tn), lambda m, n: (0, n)),
        pl.BlockSpec((1, tn), lambda m, n: (0, jnp.clip(n - _N_GATE, 0, 2 * D_MODEL // tn - 1))),
        pl.BlockSpec((1, HEAD_DIM), lambda m, n: (0, 0)),
        pl.BlockSpec((1, HEAD_DIM), lambda m, n: (0, 0)),
        pl.BlockSpec((tm, HEAD_DIM), lambda m, n: (m, 0)),
        pl.BlockSpec((tm, HEAD_DIM), lambda m, n: (m, 0)),
    ]
    out_specs = [dil_spec(1), dil_spec(4), dil_spec(16),
                 pl.BlockSpec((tm, tn), clampn(_N_QB, B_WIDTH // tn)),
                 pl.BlockSpec((tm, B_KV_WIDTH), lambda m, n: (m, 0)),
                 pl.BlockSpec((tm, B_KV_WIDTH), lambda m, n: (m, 0)),
                 pl.BlockSpec((tm, tn), clampn(_N_GATE, 2 * D_MODEL // tn))]
    out_shape = [dil_shape(1), dil_shape(4), dil_shape(16),
                 jax.ShapeDtypeStruct((T, B_WIDTH), BF16),
                 jax.ShapeDtypeStruct((T, B_KV_WIDTH), BF16),
                 jax.ShapeDtypeStruct((T, B_KV_WIDTH), BF16),
                 jax.ShapeDtypeStruct((T, 2 * D_MODEL), BF16)]
    vmem = (2 * tm * D_MODEL * 2 + 2 * D_MODEL * tn * 2 + 7 * 2 * tm * tn * 2 + 4 * tm * HEAD_DIM * 4
            + 6 * tm * tn * 4 + (8 << 20))
    return pl.pallas_call(
        _in_proj_kernel, grid=grid, in_specs=in_specs, out_specs=out_specs, out_shape=out_shape,
        scratch_shapes=[pltpu.VMEM((tn // V7X_LANES, tm, V7X_LANES), F32)],
        compiler_params=_params(2, vmem), name="in_proj",
    )(xb, w, bg, qg, kg, cos, sin_signed)


def _attn_a_kernel(q_ref, kp_ref, kc_ref, kn_ref, vp_ref, vc_ref, vn_ref, bias_ref, o_ref, lse_ref,
                   *, first_blocks, last_blocks):
    b = pl.program_id(1)
    win = A_SUB + 2 * A_HALF
    is_first = _any_eq(b, first_blocks)
    is_last = _any_eq(b, last_blocks)
    col = lax.broadcasted_iota(jnp.int32, (A_SUB, win), 1)
    lane = lax.broadcasted_iota(jnp.int32, (A_SUB, V7X_LANES), 1)
    n_sub = A_BLOCK // A_SUB

    def window(prev_ref, cur_ref, next_ref, start, hs):
        parts = []
        if start < 0:
            parts.append(prev_ref[A_HALF + start:, hs])
        parts.append(cur_ref[max(start, 0):min(start + win, A_BLOCK), hs])
        if start + win > A_BLOCK:
            parts.append(next_ref[:start + win - A_BLOCK, hs])
        return parts[0] if len(parts) == 1 else jnp.concatenate(parts, axis=0)

    for sb in range(n_sub):
        start = sb * A_SUB - A_HALF
        rows = slice(sb * A_SUB, (sb + 1) * A_SUB)
        lo = jnp.where(is_first, -start, 0) if start < 0 else 0
        hi = jnp.where(is_last, A_BLOCK - start, win) if start + win > A_BLOCK else win
        outside = jnp.logical_or(col < lo, col >= hi)
        lse_tile = jnp.zeros((A_SUB, V7X_LANES), F32)
        for h in range(A_HEADS):
            hs = slice(h * HEAD_DIM, (h + 1) * HEAD_DIM)
            q = q_ref[rows, hs]
            k = window(kp_ref, kc_ref, kn_ref, start, hs)
            v = window(vp_ref, vc_ref, vn_ref, start, hs)
            s = lax.dot_general(q, k, (((1,), (1,)), ((), ())), preferred_element_type=F32)
            s = jnp.where(outside, NEG_INF, s * ATTN_SCALE + bias_ref[h])
            m = jnp.max(s, axis=1, keepdims=True)
            p = jnp.exp(s - m)
            l = jnp.sum(p, axis=1, keepdims=True)
            o = jnp.dot(p.astype(v.dtype), v, preferred_element_type=F32)
            o_ref[rows, hs] = o / l
            lse_tile = jnp.where(lane == h, m + jnp.log(l), lse_tile)
        lse_ref[rows, :] = lse_tile


def _attn_a(a, bias, d, seqs):
    _, _, L, _ = a.shape
    nb = L // A_BLOCK
    assert all(s0 % (d * A_BLOCK) == 0 and sl % (d * A_BLOCK) == 0 for s0, sl in seqs)
    first_blocks = tuple(s0 // d // A_BLOCK for s0, _ in seqs)
    last_blocks = tuple((s0 + sl) // d // A_BLOCK - 1 for s0, sl in seqs)

    def cur(which):
        return lambda r, b: (which, r, b, 0)

    per = A_BLOCK // A_HALF

    def prev(which):
        return lambda r, b: (which, r, jnp.where(_any_eq(b, first_blocks), b * per, b * per - 1), 0)

    def nxt(which):
        return lambda r, b: (which, r, jnp.where(_any_eq(b, last_blocks), b * per, (b + 1) * per), 0)

    blk = (None, None, A_BLOCK, A_WIDTH)
    halo = (None, None, A_HALF, A_WIDTH)
    win = A_SUB + 2 * A_HALF
    in_specs = [pl.BlockSpec(blk, cur(0)),
                pl.BlockSpec(halo, prev(1)), pl.BlockSpec(blk, cur(1)), pl.BlockSpec(halo, nxt(1)),
                pl.BlockSpec(halo, prev(2)), pl.BlockSpec(blk, cur(2)), pl.BlockSpec(halo, nxt(2)),
                pl.BlockSpec((A_HEADS, A_SUB, win), lambda r, b: (0, 0, 0))]
    out_idx = lambda r, b: (r, b, 0)
    out_specs = [pl.BlockSpec((None, A_BLOCK, A_WIDTH), out_idx), pl.BlockSpec((None, A_BLOCK, V7X_LANES), out_idx)]
    out_shape = [jax.ShapeDtypeStruct((d, L, A_WIDTH), F32), jax.ShapeDtypeStruct((d, L, V7X_LANES), F32)]
    vmem = 7 * 2 * A_BLOCK * A_WIDTH * 2 + 2 * A_HEADS * A_SUB * win * 4 + 2 * A_BLOCK * A_WIDTH * 4 + (8 << 20)
    return pl.pallas_call(
        functools.partial(_attn_a_kernel, first_blocks=first_blocks, last_blocks=last_blocks),
        grid=(d, nb), in_specs=in_specs, out_specs=out_specs, out_shape=out_shape,
        compiler_params=_params(2, vmem), name=f"attn_a_d{d}",
    )(a, a, a, a, a, a, a, bias)


def _attn_b_kernel(qi_ref, ki_ref, hi_ref, fl_ref, q_ref, k_ref, v_ref, o_ref, m_sc, l_sc, acc_sc):
    step = pl.program_id(0)
    flags = fl_ref[step]

    @pl.when((flags & 1) != 0)
    def _():
        m_sc[...] = jnp.full(m_sc.shape, NEG_INF, F32)
        l_sc[...] = jnp.zeros(l_sc.shape, F32)
        acc_sc[...] = jnp.zeros(acc_sc.shape, F32)


    tq = q_ref.shape[0]
    q = jnp.concatenate([q_ref[:, g * HEAD_DIM:(g + 1) * HEAD_DIM] for g in range(B_GROUPS)], axis=0)

    n_chunks = k_ref.shape[0] // B_CK

    def chunk_rows(ci):
        return pl.ds(pl.multiple_of(ci * B_CK, B_CK), B_CK)

    def scores(ci):
        return lax.dot_general(q, k_ref[chunk_rows(ci), :], (((1,), (1,)), ((), ())),
                               preferred_element_type=F32)

    def softmax_pv(ci, t):
        v = v_ref[chunk_rows(ci), :]
        m_prev = m_sc[...]
        m_new = jnp.maximum(m_prev, jnp.max(t, axis=1, keepdims=True))
        alpha = jnp.exp2(m_prev - m_new)
        p = jnp.exp2(t - jnp.concatenate([m_new] * (B_CK // V7X_LANES), axis=1))
        l_sc[...] = alpha * l_sc[...] + jnp.sum(p, axis=1, keepdims=True)
        acc_sc[...] = alpha * acc_sc[...] + jnp.dot(p.astype(v.dtype), v, preferred_element_type=F32)
        m_sc[...] = m_new

    def chunk(ci, carry):
        softmax_pv(ci, scores(ci))
        return carry

    lax.fori_loop(0, n_chunks, chunk, 0)

    @pl.when((flags & 2) != 0)
    def _():
        for g in range(B_GROUPS):
            rows = slice(g * tq, (g + 1) * tq)
            o_ref[:, g * HEAD_DIM:(g + 1) * HEAD_DIM] = (acc_sc[rows, :] / l_sc[rows, :]).astype(o_ref.dtype)


def _attn_b_schedule(seqs, tk):
    qi, ki, hi, fl = [], [], [], []
    for s0, sl in seqs:
        nkv = sl // tk
        for h in range(B_KV_HEADS):
            for qb in range(sl // B_TQ):
                for kb in range(nkv):
                    qi.append(s0 // B_TQ + qb)
                    ki.append(s0 // tk + kb)
                    hi.append(h)
                    fl.append((1 if kb == 0 else 0) | (2 if kb == nkv - 1 else 0))
    return [np.asarray(a, np.int32) for a in (qi, ki, hi, fl)]


def _attn_b(qb, kb, vb, seqs):
    T = qb.shape[0]
    tk = min([B_TK] + [sl for _, sl in seqs])
    assert all(sl % tk == 0 and s0 % tk == 0 and sl % B_TQ == 0 for s0, sl in seqs)
    qi, ki, hi, fl = _attn_b_schedule(seqs, tk)
    gw = B_GROUPS * HEAD_DIM
    rows = B_GROUPS * B_TQ
    grid_spec = pltpu.PrefetchScalarGridSpec(
        num_scalar_prefetch=4, grid=(len(qi),),
        in_specs=[pl.BlockSpec((B_TQ, gw), lambda s, qi, ki, hi, fl: (qi[s], hi[s])),
                  pl.BlockSpec((tk, HEAD_DIM), lambda s, qi, ki, hi, fl: (ki[s], hi[s])),
                  pl.BlockSpec((tk, HEAD_DIM), lambda s, qi, ki, hi, fl: (ki[s], hi[s]))],
        out_specs=pl.BlockSpec((B_TQ, gw), lambda s, qi, ki, hi, fl: (qi[s], hi[s])),
        scratch_shapes=[pltpu.VMEM((rows, V7X_LANES), F32)] * 3)
    vmem = 4 * B_TQ * gw * 2 + 4 * tk * HEAD_DIM * 2 + 3 * rows * V7X_LANES * 4 + 4 * rows * B_CK * 4 + (8 << 20)
    return pl.pallas_call(
        _attn_b_kernel, grid_spec=grid_spec, out_shape=jax.ShapeDtypeStruct((T, B_WIDTH), BF16),
        compiler_params=_params(1, vmem), name="attn_b",
    )(jnp.asarray(qi), jnp.asarray(ki), jnp.asarray(hi), jnp.asarray(fl), qb, kb, vb)


def _branch_kernel(*refs, with_router):
    (o1_ref, o4_ref, o16_ref, l1_ref, l4_ref, l16_ref, ob_ref, sga_ref, sgb_ref, x_ref,
     wba_ref, wbb_ref, wo_ref, g_ref, b_ref) = refs[:15]
    rest = refs[15:]
    if with_router:
        rw_ref, x1_ref, x1c_ref, route_ref, o4_sc, o16_sc, l4_sc, l16_sc = rest
    else:
        x1_ref, x1b_ref, o4_sc, o16_sc, l4_sc, l16_sc = rest
    tm = x_ref.shape[0]
    for d, src, dst, lsrc, ldst in ((4, o4_ref, o4_sc, l4_ref, l4_sc), (16, o16_ref, o16_sc, l16_ref, l16_sc)):
        for r in range(d):
            ldst[pl.ds(r, tm // d, stride=d), :] = lsrc[r]
            for h in range(A_HEADS):
                dst[h, pl.ds(r, tm // d, stride=d), :] = src[r, :, h * HEAD_DIM:(h + 1) * HEAD_DIM]
    l1, l4, l16 = l1_ref[0], l4_sc[...], l16_sc[...]
    mx = jnp.maximum(jnp.maximum(l1, l4), l16)
    e1, e4, e16 = jnp.exp(l1 - mx), jnp.exp(l4 - mx), jnp.exp(l16 - mx)
    den = e1 + e4 + e16
    w1, w4, w16 = e1 / den, e4 / den, e16 / den
    parts = []
    for h in range(A_HEADS):
        hs = slice(h * HEAD_DIM, (h + 1) * HEAD_DIM)
        oa_h = w1[:, h:h + 1] * o1_ref[0, :, hs] + w4[:, h:h + 1] * o4_sc[h] + w16[:, h:h + 1] * o16_sc[h]
        parts.append(oa_h.astype(BF16))
    oa = jnp.concatenate(parts, axis=1)
    ya = jnp.dot(oa, wba_ref[...], preferred_element_type=F32)
    yb = jnp.dot(ob_ref[...], wbb_ref[...], preferred_element_type=F32)
    merged = sga_ref[...].astype(F32) * ya + sgb_ref[...].astype(F32) * yb
    y = jnp.dot(merged.astype(BF16), wo_ref[...], preferred_element_type=F32)
    out = _layer_norm_rows(DEEPNORM_ALPHA * x_ref[...] + y, g_ref[...], b_ref[...])
    x1_ref[...] = out
    if not with_router:
        x1b_ref[...] = out.astype(BF16)
    if with_router:
        for c in range(ROW_SLAB):
            x1c_ref[pl.ds(c, tm, stride=ROW_SLAB), :] = out[:, c * V7X_LANES:(c + 1) * V7X_LANES]
        out_hi = out.astype(BF16)
        out_lo = (out - out_hi.astype(F32)).astype(BF16)
        both = jnp.dot(out_hi, rw_ref[...], preferred_element_type=F32)
        logits = (both[:, :V7X_LANES] + both[:, V7X_LANES:]
                  + jnp.dot(out_lo, rw_ref[:, :V7X_LANES], preferred_element_type=F32))
        lane = lax.broadcasted_iota(jnp.int32, logits.shape, 1)
        logits = jnp.where(lane < N_EXPERTS, logits, -jnp.inf)
        v1 = jnp.max(logits, axis=1, keepdims=True)
        i1 = jnp.min(jnp.where(logits == v1, lane, V7X_LANES), axis=1, keepdims=True)
        rem = jnp.where(lane == i1, -jnp.inf, logits)
        v2 = jnp.max(rem, axis=1, keepdims=True)
        i2 = jnp.min(jnp.where(rem == v2, lane, V7X_LANES), axis=1, keepdims=True)
        e2 = jnp.exp(v2 - v1)
        g1 = 1.0 / (1.0 + e2)
        g2 = e2 / (1.0 + e2)
        route = jnp.where(lane == 0, i1.astype(F32),
                          jnp.where(lane == 1, i2.astype(F32),
                                    jnp.where(lane == 2, g1, jnp.where(lane == 3, g2, 0.0))))
        route_ref[...] = route


def _branch(o1, o4, o16, l1, l4, l16, ob, gates, x, wba, wbb, wo, g, b, router_w=None):
    T = x.shape[0]
    tm = BRANCH_TM
    with_router = router_w is not None
    row = lambda m: (m, 0)
    const = lambda m: (0, 0)
    single = pl.Buffered(1)
    in_specs = [
        pl.BlockSpec((1, tm, A_WIDTH), lambda m: (0, m, 0)),
        pl.BlockSpec((4, tm // 4, A_WIDTH), lambda m: (0, m, 0)),
        pl.BlockSpec((16, tm // 16, A_WIDTH), lambda m: (0, m, 0)),
        pl.BlockSpec((1, tm, V7X_LANES), lambda m: (0, m, 0)),
        pl.BlockSpec((4, tm // 4, V7X_LANES), lambda m: (0, m, 0)),
        pl.BlockSpec((16, tm // 16, V7X_LANES), lambda m: (0, m, 0)),
        pl.BlockSpec((tm, B_WIDTH), row),
        pl.BlockSpec((tm, D_MODEL), lambda m: (m, 0)),
        pl.BlockSpec((tm, D_MODEL), lambda m: (m, 1)),
        pl.BlockSpec((tm, D_MODEL), row),
        pl.BlockSpec((A_WIDTH, D_MODEL), const, pipeline_mode=single),
        pl.BlockSpec((B_WIDTH, D_MODEL), const, pipeline_mode=single),
        pl.BlockSpec((D_MODEL, D_MODEL), const, pipeline_mode=single),
        pl.BlockSpec((1, D_MODEL), const),
        pl.BlockSpec((1, D_MODEL), const),
    ]
    args = [o1, o4, o16, l1, l4, l16, ob, gates, gates, x, wba, wbb, wo, g, b]
    out_specs = [pl.BlockSpec((tm, D_MODEL), row)]
    out_shape = [jax.ShapeDtypeStruct((T, D_MODEL), F32)]
    if with_router:
        in_specs.append(pl.BlockSpec((D_MODEL, 2 * V7X_LANES), const, pipeline_mode=single))
        args.append(router_w)
        out_specs += [pl.BlockSpec((tm * ROW_SLAB, V7X_LANES), row), pl.BlockSpec((tm, V7X_LANES), row)]
        out_shape += [jax.ShapeDtypeStruct((T * ROW_SLAB, V7X_LANES), F32),
                      jax.ShapeDtypeStruct((T, V7X_LANES), F32)]
    else:
        out_specs.append(pl.BlockSpec((tm, D_MODEL), row))
        out_shape.append(jax.ShapeDtypeStruct((T, D_MODEL), BF16))
    scratch = [pltpu.VMEM((A_HEADS, tm, HEAD_DIM), F32), pltpu.VMEM((A_HEADS, tm, HEAD_DIM), F32),
               pltpu.VMEM((tm, V7X_LANES), F32), pltpu.VMEM((tm, V7X_LANES), F32)]
    vmem = ((A_WIDTH + B_WIDTH + D_MODEL) * D_MODEL * 2 + D_MODEL * V7X_LANES * 4
            + 2 * 3 * tm * A_WIDTH * 4 + 2 * tm * A_WIDTH * 4 + 2 * tm * B_WIDTH * 2 + 4 * tm * D_MODEL * 2
            + 2 * tm * D_MODEL * 4 + 2 * tm * D_MODEL * 6 + 8 * tm * D_MODEL * 4 + (8 << 20))
    return pl.pallas_call(
        functools.partial(_branch_kernel, with_router=with_router),
        grid=(T // tm,), in_specs=in_specs, out_specs=out_specs, out_shape=out_shape,
        scratch_shapes=scratch, compiler_params=_params(1, vmem),
        name="branch_router" if with_router else "branch",
    )(*args)


def _swiglu_accumulate(xb, wg_ref, wu_ref, wd_ref, acc_ref):
    g = jnp.dot(xb, wg_ref[...], preferred_element_type=F32)
    u = jnp.dot(xb, wu_ref[...], preferred_element_type=F32)
    a = (g * jax.nn.sigmoid(g) * u).astype(BF16)
    acc_ref[...] += jnp.dot(a, wd_ref[...], preferred_element_type=F32)


def _ffn_dense_kernel(xb_ref, x_ref, wg_ref, wu_ref, wd_ref, g_ref, b_ref, o_ref, ob_ref):
    j = pl.program_id(1)

    @pl.when(j == 0)
    def _():
        o_ref[...] = jnp.zeros(o_ref.shape, F32)

    _swiglu_accumulate(xb_ref[...], wg_ref, wu_ref, wd_ref, o_ref)

    @pl.when(j == pl.num_programs(1) - 1)
    def _():
        out = _layer_norm_rows(DEEPNORM_ALPHA * x_ref[...] + o_ref[...], g_ref[...], b_ref[...])
        o_ref[...] = out
        ob_ref[...] = out.astype(BF16)


def _ffn_dense(xb, x, wg, wu, wd, g, b):
    T = x.shape[0]
    tm, tf = FFN_TM, FFN_TF
    row = lambda m, j: (m, 0)
    const = lambda m, j: (0, 0)
    in_specs = [pl.BlockSpec((tm, D_MODEL), row), pl.BlockSpec((tm, D_MODEL), row),
                pl.BlockSpec((D_MODEL, tf), lambda m, j: (0, j)),
                pl.BlockSpec((D_MODEL, tf), lambda m, j: (0, j)),
                pl.BlockSpec((tf, D_MODEL), lambda m, j: (j, 0)),
                pl.BlockSpec((1, D_MODEL), const), pl.BlockSpec((1, D_MODEL), const)]
    out_specs = [pl.BlockSpec((tm, D_MODEL), row), pl.BlockSpec((tm, D_MODEL), row)]
    out_shape = [jax.ShapeDtypeStruct((T, D_MODEL), F32), jax.ShapeDtypeStruct((T, D_MODEL), BF16)]
    vmem = (2 * tm * D_MODEL * 2 + 2 * tm * D_MODEL * 4 + 2 * 3 * D_MODEL * tf * 2 + 2 * tm * D_MODEL * 6
            + 4 * tm * tf * 4 + 2 * tm * D_MODEL * 4 + (8 << 20))
    return pl.pallas_call(
        _ffn_dense_kernel, grid=(T // tm, D_FF // tf), in_specs=in_specs, out_specs=out_specs,
        out_shape=out_shape, compiler_params=_params(2, vmem), name="ffn_dense",
    )(xb, x, wg, wu, wd, g, b)


def _moe_gather_kernel(src_ref, x_hbm, o_ref, slab_sc, sem):
    i = pl.program_id(0)
    n_tiles = pl.num_programs(0) - 1
    tm = src_ref.shape[2]
    slot = lax.rem(i, 2)

    def slab_copy(s, j, t):
        return pltpu.make_async_copy(x_hbm.at[pl.ds(t * ROW_SLAB, ROW_SLAB), :],
                                     slab_sc.at[s, pl.ds(j * ROW_SLAB, ROW_SLAB), :], sem.at[s])

    @pl.when(i < n_tiles)
    def _():
        def issue(g, c):
            for u in range(DMA_UNROLL):
                j = g * DMA_UNROLL + u
                slab_copy(slot, j, src_ref[0, 0, j]).start(priority=u % 2)
            return c
        lax.fori_loop(0, tm // DMA_UNROLL, issue, 0)

    @pl.when(i > 0)
    def _():
        done = 1 - slot

        def wait(j, c):
            slab_copy(done, j, 0).wait()
            return c
        lax.fori_loop(0, tm, wait, 0, unroll=8)
        for c in range(ROW_SLAB):
            cs = slice(c * V7X_LANES, (c + 1) * V7X_LANES)
            o_ref[:, cs] = slab_sc[done, pl.ds(c, tm, stride=ROW_SLAB), :].astype(o_ref.dtype)


def _moe_gather(xc, src):
    n_tiles, _, tm = src.shape
    vmem = 2 * tm * D_MODEL * 4 + 2 * tm * D_MODEL * 2 + 2 * tm * D_MODEL * 4 + (8 << 20)
    return pl.pallas_call(
        _moe_gather_kernel, grid=(n_tiles + 1,),
        in_specs=[pl.BlockSpec((1, 1, tm), lambda i: (jnp.minimum(i, n_tiles - 1), 0, 0), memory_space=pltpu.SMEM),
                  pl.BlockSpec(memory_space=pl.ANY)],
        out_specs=pl.BlockSpec((tm, D_MODEL), lambda i: (jnp.maximum(i - 1, 0), 0)),
        out_shape=jax.ShapeDtypeStruct((n_tiles * tm, D_MODEL), BF16),
        scratch_shapes=[pltpu.VMEM((2, tm * ROW_SLAB, V7X_LANES), F32), pltpu.SemaphoreType.DMA((2,))],
        compiler_params=_params(1, vmem), name="moe_gather",
    )(src, xc)


def _moe_ffn_kernel(te_ref, tv_ref, x_ref, wg_ref, wu_ref, wd_ref, o_ref):
    i = pl.program_id(0)
    j = pl.program_id(1)

    @pl.when(j == 0)
    def _():
        o_ref[...] = jnp.zeros(o_ref.shape, F32)

    @pl.when(tv_ref[i] != 0)
    def _():
        _swiglu_accumulate(x_ref[...], wg_ref, wu_ref, wd_ref, o_ref)


def _moe_ffn(xs, tile_expert, tile_valid, wg, wu, wd):
    P = xs.shape[0]
    tm, tf = MOE_TM, MOE_TF
    nj = D_FF_EXPERT // tf

    def jeff(i, j, tv):
        return jnp.where(tv[i] != 0, j, nj - 1)

    grid_spec = pltpu.PrefetchScalarGridSpec(
        num_scalar_prefetch=2, grid=(P // tm, nj),
        in_specs=[pl.BlockSpec((tm, D_MODEL), lambda i, j, te, tv: (i, 0)),
                  pl.BlockSpec((None, D_MODEL, tf), lambda i, j, te, tv: (te[i], 0, jeff(i, j, tv))),
                  pl.BlockSpec((None, D_MODEL, tf), lambda i, j, te, tv: (te[i], 0, jeff(i, j, tv))),
                  pl.BlockSpec((None, tf, D_MODEL), lambda i, j, te, tv: (te[i], jeff(i, j, tv), 0))],
        out_specs=pl.BlockSpec((tm, D_MODEL), lambda i, j, te, tv: (i, 0)))
    vmem = (2 * tm * D_MODEL * 2 + 2 * 3 * D_MODEL * tf * 2 + 2 * tm * D_MODEL * 4
            + 4 * tm * tf * 4 + 2 * tm * D_MODEL * 4 + (8 << 20))
    return pl.pallas_call(
        _moe_ffn_kernel, grid_spec=grid_spec, out_shape=jax.ShapeDtypeStruct((P, D_MODEL), F32),
        compiler_params=_params(2, vmem), name="moe_ffn",
    )(tile_expert, tile_valid, xs, wg, wu, wd)


def _moe_combine_kernel(dest_ref, y_hbm, route_ref, x_ref, g_ref, b_ref, o_ref, ob_ref, y0_sc, y1_sc, sem):
    tm = x_ref.shape[0]

    def row_copy(j, k, p):
        dst = y0_sc if k == 0 else y1_sc
        return pltpu.make_async_copy(y_hbm.at[pl.ds(p, 1), :], dst.at[pl.ds(j, 1), :], sem)

    def issue(j, c):
        row_copy(j, 0, dest_ref[0, 0, 2 * j]).start(priority=0)
        row_copy(j, 1, dest_ref[0, 0, 2 * j + 1]).start(priority=1)
        return c

    def wait(j, c):
        row_copy(j, 0, 0).wait()
        row_copy(j, 1, 0).wait()
        return c

    lax.fori_loop(0, tm, issue, 0, unroll=8)
    lax.fori_loop(0, tm, wait, 0, unroll=8)
    route = route_ref[...]
    f = route[:, 2:3] * y0_sc[...] + route[:, 3:4] * y1_sc[...]
    out = _layer_norm_rows(DEEPNORM_ALPHA * x_ref[...] + f, g_ref[...], b_ref[...])
    o_ref[...] = out
    ob_ref[...] = out.astype(BF16)


def _moe_combine(y, dest, route, x, g, b):
    T = x.shape[0]
    tm = COMBINE_TM
    row = lambda m: (m, 0)
    const = lambda m: (0, 0)
    vmem = 2 * tm * D_MODEL * 4 + 2 * tm * D_MODEL * 4 + 2 * tm * D_MODEL * 6 + 6 * tm * D_MODEL * 4 + (8 << 20)
    return pl.pallas_call(
        _moe_combine_kernel, grid=(T // tm,),
        in_specs=[pl.BlockSpec((1, 1, 2 * tm), lambda m: (m, 0, 0), memory_space=pltpu.SMEM),
                  pl.BlockSpec(memory_space=pl.ANY),
                  pl.BlockSpec((tm, V7X_LANES), row),
                  pl.BlockSpec((tm, D_MODEL), row),
                  pl.BlockSpec((1, D_MODEL), const), pl.BlockSpec((1, D_MODEL), const)],
        out_specs=[pl.BlockSpec((tm, D_MODEL), row), pl.BlockSpec((tm, D_MODEL), row)],
        out_shape=[jax.ShapeDtypeStruct((T, D_MODEL), F32), jax.ShapeDtypeStruct((T, D_MODEL), BF16)],
        scratch_shapes=[pltpu.VMEM((tm, D_MODEL), F32), pltpu.VMEM((tm, D_MODEL), F32),
                        pltpu.SemaphoreType.DMA(())],
        compiler_params=_params(1, vmem), name="moe_combine",
    )(dest.reshape(T // tm, 1, 2 * tm), y, route, x, g, b)


def _route_plan(route, tm):
    T = route.shape[0]
    n_assign = T * TOP_K
    n_tiles = n_assign // tm + N_EXPERTS
    flat_e = route[:, :TOP_K].astype(jnp.int32).reshape(n_assign)
    onehot = (jnp.arange(N_EXPERTS, dtype=jnp.int32)[:, None] == flat_e[None, :]).astype(F32)
    blocks = onehot.reshape(N_EXPERTS, n_assign // V7X_LANES, V7X_LANES)
    tri = (jnp.arange(V7X_LANES)[:, None] >= jnp.arange(V7X_LANES)[None, :]).astype(F32)
    within = jnp.einsum("ebj,ij->ebi", blocks, tri)
    totals = within[:, :, -1]
    running = jnp.cumsum(totals, axis=1)
    csum = (within + (running - totals)[:, :, None]).reshape(N_EXPERTS, n_assign)
    counts = running[:, -1].astype(jnp.int32)
    tiles_per = (counts + tm - 1) // tm
    tile_end = jnp.cumsum(tiles_per)
    row_start = (tile_end - tiles_per) * tm
    dest = jnp.sum(onehot * (csum - 1.0 + row_start.astype(F32)[:, None]), axis=0).astype(jnp.int32)
    tile_ids = jnp.arange(n_tiles, dtype=jnp.int32)
    n_valid = tile_end[-1]
    tile_valid = (tile_ids < n_valid).astype(jnp.int32)
    owner = jnp.sum((jnp.minimum(tile_ids, n_valid - 1)[:, None] >= tile_end[None, :]).astype(jnp.int32), axis=1)
    tile_expert = jnp.minimum(owner, N_EXPERTS - 1).astype(jnp.int32)
    src = jnp.zeros((n_tiles * tm,), jnp.int32).at[dest].set(jnp.arange(n_assign, dtype=jnp.int32) // TOP_K)
    return dest.reshape(T, TOP_K), src.reshape(n_tiles, 1, tm), tile_expert, tile_valid


def _t5_bucket(rel):
    nb = N_BUCKETS // 2
    max_exact = nb // 2
    n = jnp.abs(rel)
    large = max_exact + (jnp.log(jnp.maximum(n, 1).astype(F32) / max_exact)
                         / math.log(MAX_DISTANCE / max_exact) * (nb - max_exact)).astype(jnp.int32)
    large = jnp.minimum(large, nb - 1)
    return jnp.where(rel > 0, nb, 0) + jnp.where(n < max_exact, n, large)


def _a_bias_table(rel_bias, d):
    i = jnp.arange(A_SUB, dtype=jnp.int32)[:, None]
    j = jnp.arange(A_SUB + 2 * A_HALF, dtype=jnp.int32)[None, :]
    rel = j - A_HALF - i
    bucket = _t5_bucket(rel * d)
    hit = bucket[None, :, :, None] == jnp.arange(N_BUCKETS, dtype=jnp.int32)
    bias = jnp.sum(jnp.where(hit, rel_bias.astype(F32).T[:, None, None, :], 0.0), axis=-1)
    return jnp.where((jnp.abs(rel) <= A_HALF)[None], bias, NEG_INF)


def _rope_tables(seqs):
    pos = jnp.concatenate([jnp.arange(sl, dtype=jnp.int32) for _, sl in seqs])
    row = (pos // GRID_W).astype(F32)
    col = (pos % GRID_W).astype(F32)
    n_freq = HEAD_DIM // 4
    inv_freq = ROPE_THETA ** (-jnp.arange(n_freq, dtype=F32) / n_freq)
    ang_r = row[:, None] * inv_freq[None, :]
    ang_c = col[:, None] * inv_freq[None, :]
    ang = jnp.concatenate([ang_r, ang_r, ang_c, ang_c], axis=-1)
    sign = jnp.where((jnp.arange(HEAD_DIM) % (HEAD_DIM // 2)) < (HEAD_DIM // 4), -1.0, 1.0).astype(F32)
    return jnp.cos(ang), jnp.sin(ang) * sign[None, :]


def kernel(x_prompt, x_sample, w_in, b_gate, q_norm_g, k_norm_g, rel_bias, w_branch_a, w_branch_b, w_out,
           ln1_g, ln1_b, ln2_g, ln2_b, ffn_w_gate, ffn_w_up, ffn_w_down, router_w, exp_w_gate, exp_w_up,
           exp_w_down):
    seqs = []
    for xs in (x_prompt, x_sample):
        for _ in range(xs.shape[0]):
            seqs.append((sum(sl for _, sl in seqs), xs.shape[1]))
    seqs = tuple(seqs)
    T = sum(sl for _, sl in seqs)
    assert all(s0 % max(PROJ_TM, B_TQ) == 0 and sl % max(PROJ_TM, B_TQ) == 0 for s0, sl in seqs)

    x = jnp.concatenate([x_prompt.reshape(-1, D_MODEL), x_sample.reshape(-1, D_MODEL)], axis=0)
    xb = x.astype(BF16)
    cos, sin_signed = _rope_tables(seqs)
    biases = [_a_bias_table(rel_bias, d) for _, d in DILATED_PATTERNS]
    router_pad = jnp.pad(router_w.astype(F32), ((0, 0), (0, 0), (0, V7X_LANES - N_EXPERTS)))
    router_hi = router_pad.astype(BF16)
    router_lo = (router_pad - router_hi.astype(F32)).astype(BF16)
    router_split = jnp.concatenate([router_hi, router_lo], axis=-1)

    for l in range(DEPTH):
        a1, a4, a16, qb, kb, vb, gates = _in_proj(
            xb, w_in[l].astype(BF16), b_gate[l].reshape(1, 2 * D_MODEL),
            q_norm_g[l].reshape(1, HEAD_DIM), k_norm_g[l].reshape(1, HEAD_DIM), cos, sin_signed)
        o1, l1 = _attn_a(a1, biases[0], 1, seqs)
        o4, l4 = _attn_a(a4, biases[1], 4, seqs)
        o16, l16 = _attn_a(a16, biases[2], 16, seqs)
        ob = _attn_b(qb, kb, vb, seqs)
        moe = l % 2 == 1
        res = _branch(o1, o4, o16, l1, l4, l16, ob, gates, x,
                      w_branch_a[l].astype(BF16), w_branch_b[l].astype(BF16), w_out[l].astype(BF16),
                      ln1_g[l].reshape(1, D_MODEL), ln1_b[l].reshape(1, D_MODEL),
                      router_split[l // 2] if moe else None)
        g2, b2 = ln2_g[l].reshape(1, D_MODEL), ln2_b[l].reshape(1, D_MODEL)
        j = l // 2
        if not moe:
            x1, x1b = res
            x, xb = _ffn_dense(x1b, x1, ffn_w_gate[j].astype(BF16), ffn_w_up[j].astype(BF16),
                               ffn_w_down[j].astype(BF16), g2, b2)
        else:
            x1, x1c, route = res
            dest, src, tile_expert, tile_valid = _route_plan(route, MOE_TM)
            xs = _moe_gather(x1c, src)
            ys = _moe_ffn(xs, tile_expert, tile_valid, exp_w_gate[j].astype(BF16), exp_w_up[j].astype(BF16),
                          exp_w_down[j].astype(BF16))
            x, xb = _moe_combine(ys, dest, route, x1, g2, b2)

    n_prompt = x_prompt.shape[0] * x_prompt.shape[1]
    return (x[:n_prompt].reshape(x_prompt.shape), x[n_prompt:].reshape(x_sample.shape))
```

```python
import functools
import math

import numpy as np
import jax
import jax.numpy as jnp
from jax import lax
from jax.experimental import pallas as pl
from jax.experimental.pallas import tpu as pltpu

D_MODEL = 2048
DEPTH = 2
HEAD_DIM = 128
A_HEADS = 8
A_WIDTH = A_HEADS * HEAD_DIM
DILATED_PATTERNS = ((128, 1), (512, 4), (2048, 16))
A_HALF = 64
B_HEADS = 8
B_KV_HEADS = 2
B_GROUPS = B_HEADS // B_KV_HEADS
B_WIDTH = B_HEADS * HEAD_DIM
B_KV_WIDTH = B_KV_HEADS * HEAD_DIM
GRID_W = 64
ROPE_THETA = 10000.0
N_BUCKETS = 32
MAX_DISTANCE = 1024
IN_COLS = 3 * A_WIDTH + B_WIDTH + 2 * B_KV_WIDTH + 2 * D_MODEL
D_FF = 5632
N_EXPERTS = 8
TOP_K = 2
D_FF_EXPERT = 7168
DEEPNORM_ALPHA = (2.0 * DEPTH) ** 0.25
LN_EPS = 1e-5
RMS_EPS = 1e-6
NEG_INF = -1e30
ATTN_SCALE = HEAD_DIM ** -0.5

V7X_LANES = 128
V7X_VMEM_BYTES = 64 * 1024 * 1024
ROW_SLAB = D_MODEL // V7X_LANES

PROJ_TM = 1024
PROJ_TN = 512
A_BLOCK = 256
A_SUB = 128
B_TQ = 1024
B_TK = 4096
B_CK = 512
BRANCH_TM = 256
FFN_TM = 512
FFN_TF = 512
MOE_TM = 512
MOE_TF = 512
COMBINE_TM = 256
DMA_UNROLL = 8

BF16 = jnp.bfloat16
F32 = jnp.float32

assert all(w // (2 * d) == A_HALF for w, d in DILATED_PATTERNS)
assert IN_COLS % PROJ_TN == 0 and D_FF % FFN_TF == 0 and D_FF_EXPERT % MOE_TF == 0


def _vmem_limit(nbytes):
    return int(min(nbytes, V7X_VMEM_BYTES - 4 * 1024 * 1024))


def _params(n_axes, vmem_bytes):
    return pltpu.CompilerParams(
        dimension_semantics=("arbitrary",) * n_axes, vmem_limit_bytes=_vmem_limit(vmem_bytes))


def _any_eq(b, values):
    return functools.reduce(jnp.logical_or, [b == v for v in values])


def _layer_norm_rows(z, g, b):
    mu = jnp.mean(z, axis=-1, keepdims=True)
    zc = z - mu
    var = jnp.mean(zc * zc, axis=-1, keepdims=True)
    return zc * lax.rsqrt(var + LN_EPS) * g + b


_N_QA, _N_KA, _N_VA = 0, A_WIDTH // PROJ_TN, 2 * A_WIDTH // PROJ_TN
_N_QB = 3 * A_WIDTH // PROJ_TN
_N_KVB = _N_QB + B_WIDTH // PROJ_TN
_N_GATE = _N_KVB + 1
_N_TILES = IN_COLS // PROJ_TN
assert 2 * B_KV_WIDTH == PROJ_TN and A_WIDTH % PROJ_TN == 0


def _rms_rope_store(acc, n_heads, g, cos, sin_signed, post_scale, out_ref):
    lane = lax.broadcasted_iota(jnp.int32, (acc.shape[0], HEAD_DIM), 1)
    low = (lane % (HEAD_DIM // 2)) < (HEAD_DIM // 4)
    for h in range(n_heads):
        xh = acc[:, h * HEAD_DIM:(h + 1) * HEAD_DIM]
        y = xh * lax.rsqrt(jnp.mean(xh * xh, axis=-1, keepdims=True) + RMS_EPS) * g
        rot = jnp.where(low, pltpu.roll(y, HEAD_DIM - HEAD_DIM // 4, 1), pltpu.roll(y, HEAD_DIM // 4, 1))
        roped = y * cos + rot * sin_signed
        if post_scale != 1.0:
            roped = roped * post_scale
        out_ref[:, h * HEAD_DIM:(h + 1) * HEAD_DIM] = roped.astype(out_ref.dtype)


B_Q_PRESCALE = ATTN_SCALE * math.log2(math.e)


def _in_proj_kernel(x_ref, w_ref, bg_ref, qg_ref, kg_ref, cos_ref, sin_ref,
                    a1_ref, a4_ref, a16_ref, qb_ref, kb_ref, vb_ref, gate_ref, acc_ref):
    n = pl.program_id(1)
    tm = x_ref.shape[0]

    def product():
        return jnp.dot(x_ref[...], w_ref[...], preferred_element_type=F32)

    @pl.when(n < _N_QB)
    def _():
        acc = product()
        a1_ref[0] = acc.astype(a1_ref.dtype)
        for c in range(acc_ref.shape[0]):
            cs = slice(c * V7X_LANES, (c + 1) * V7X_LANES)
            acc_ref[c] = acc[:, cs]
            for d, o in ((4, a4_ref), (16, a16_ref)):
                for r in range(d):
                    o[r, :, cs] = acc_ref[c, pl.ds(r, tm // d, stride=d), :].astype(o.dtype)

    @pl.when(jnp.logical_and(n >= _N_QB, n < _N_KVB))
    def _():
        _rms_rope_store(product(), PROJ_TN // HEAD_DIM, qg_ref[...], cos_ref[...], sin_ref[...], B_Q_PRESCALE,
                        qb_ref)

    @pl.when(n == _N_KVB)
    def _():
        acc = product()
        _rms_rope_store(acc[:, :B_KV_WIDTH], B_KV_HEADS, kg_ref[...], cos_ref[...], sin_ref[...], 1.0, kb_ref)
        vb_ref[...] = acc[:, B_KV_WIDTH:].astype(vb_ref.dtype)

    @pl.when(n >= _N_GATE)
    def _():
        gate_ref[...] = jax.nn.sigmoid(product() + bg_ref[...]).astype(gate_ref.dtype)


def _in_proj(xb, w, bg, qg, kg, cos, sin_signed):
    T = xb.shape[0]
    tm, tn = PROJ_TM, PROJ_TN
    grid = (T // tm, _N_TILES)
    a_tiles = A_WIDTH // tn

    def clampn(lo, cnt):
        return lambda m, n: (m, jnp.clip(n - lo, 0, cnt - 1))

    def dil_spec(d):
        def index(m, n):
            nc = jnp.minimum(n, 3 * a_tiles - 1)
            return (nc // a_tiles, 0, m, nc % a_tiles)
        return pl.BlockSpec((None, d, tm // d, tn), index)

    def dil_shape(d):
        return jax.ShapeDtypeStruct((3, d, T // d, A_WIDTH), BF16)

    in_specs = [
        pl.BlockSpec((tm, D_MODEL), lambda m, n: (m, 0)),
        pl.BlockSpec((D_MODEL, tn), lambda m, n: (0, n)),
        pl.BlockSpec((1, tn), lambda m, n: (0, jnp.clip(n - _N_GATE, 0, 2 * D_MODEL // tn - 1))),
        pl.BlockSpec((1, HEAD_DIM), lambda m, n: (0, 0)),
        pl.BlockSpec((1, HEAD_DIM), lambda m, n: (0, 0)),
        pl.BlockSpec((tm, HEAD_DIM), lambda m, n: (m, 0)),
        pl.BlockSpec((tm, HEAD_DIM), lambda m, n: (m, 0)),
    ]
    out_specs = [dil_spec(1), dil_spec(4), dil_spec(16),
                 pl.BlockSpec((tm, tn), clampn(_N_QB, B_WIDTH // tn)),
                 pl.BlockSpec((tm, B_KV_WIDTH), lambda m, n: (m, 0)),
                 pl.BlockSpec((tm, B_KV_WIDTH), lambda m, n: (m, 0)),
                 pl.BlockSpec((tm, tn), clampn(_N_GATE, 2 * D_MODEL // tn))]
    out_shape = [dil_shape(1), dil_shape(4), dil_shape(16),
                 jax.ShapeDtypeStruct((T, B_WIDTH), BF16),
                 jax.ShapeDtypeStruct((T, B_KV_WIDTH), BF16),
                 jax.ShapeDtypeStruct((T, B_KV_WIDTH), BF16),
                 jax.ShapeDtypeStruct((T, 2 * D_MODEL), BF16)]
    vmem = (2 * tm * D_MODEL * 2 + 2 * D_MODEL * tn * 2 + 7 * 2 * tm * tn * 2 + 4 * tm * HEAD_DIM * 4
            + 6 * tm * tn * 4 + (8 << 20))
    return pl.pallas_call(
        _in_proj_kernel, grid=grid, in_specs=in_specs, out_specs=out_specs, out_shape=out_shape,
        scratch_shapes=[pltpu.VMEM((tn // V7X_LANES, tm, V7X_LANES), F32)],
        compiler_params=_params(2, vmem), name="in_proj",
    )(xb, w, bg, qg, kg, cos, sin_signed)


def _attn_a_kernel(q_ref, kp_ref, kc_ref, kn_ref, vp_ref, vc_ref, vn_ref, bias_ref, o_ref, lse_ref,
                   *, first_blocks, last_blocks):
    b = pl.program_id(1)
    win = A_SUB + 2 * A_HALF
    is_first = _any_eq(b, first_blocks)
    is_last = _any_eq(b, last_blocks)
    col = lax.broadcasted_iota(jnp.int32, (A_SUB, win), 1)
    lane = lax.broadcasted_iota(jnp.int32, (A_SUB, V7X_LANES), 1)
    n_sub = A_BLOCK // A_SUB

    def window(prev_ref, cur_ref, next_ref, start, hs):
        parts = []
        if start < 0:
            parts.append(prev_ref[A_HALF + start:, hs])
        parts.append(cur_ref[max(start, 0):min(start + win, A_BLOCK), hs])
        if start + win > A_BLOCK:
            parts.append(next_ref[:start + win - A_BLOCK, hs])
        return parts[0] if len(parts) == 1 else jnp.concatenate(parts, axis=0)

    for sb in range(n_sub):
        start = sb * A_SUB - A_HALF
        rows = slice(sb * A_SUB, (sb + 1) * A_SUB)
        lo = jnp.where(is_first, -start, 0) if start < 0 else 0
        hi = jnp.where(is_last, A_BLOCK - start, win) if start + win > A_BLOCK else win
        outside = jnp.logical_or(col < lo, col >= hi)
        lse_tile = jnp.zeros((A_SUB, V7X_LANES), F32)
        for h in range(A_HEADS):
            hs = slice(h * HEAD_DIM, (h + 1) * HEAD_DIM)
            q = q_ref[rows, hs]
            k = window(kp_ref, kc_ref, kn_ref, start, hs)
            v = window(vp_ref, vc_ref, vn_ref, start, hs)
            s = lax.dot_general(q, k, (((1,), (1,)), ((), ())), preferred_element_type=F32)
            s = jnp.where(outside, NEG_INF, s * ATTN_SCALE + bias_ref[h])
            m = jnp.max(s, axis=1, keepdims=True)
            p = jnp.exp(s - m)
            l = jnp.sum(p, axis=1, keepdims=True)
            o = jnp.dot(p.astype(v.dtype), v, preferred_element_type=F32)
            o_ref[rows, hs] = o / l
            lse_tile = jnp.where(lane == h, m + jnp.log(l), lse_tile)
        lse_ref[rows, :] = lse_tile


def _attn_a(a, bias, d, seqs):
    _, _, L, _ = a.shape
    nb = L // A_BLOCK
    assert all(s0 % (d * A_BLOCK) == 0 and sl % (d * A_BLOCK) == 0 for s0, sl in seqs)
    first_blocks = tuple(s0 // d // A_BLOCK for s0, _ in seqs)
    last_blocks = tuple((s0 + sl) // d // A_BLOCK - 1 for s0, sl in seqs)

    def cur(which):
        return lambda r, b: (which, r, b, 0)

    per = A_BLOCK // A_HALF

    def prev(which):
        return lambda r, b: (which, r, jnp.where(_any_eq(b, first_blocks), b * per, b * per - 1), 0)

    def nxt(which):
        return lambda r, b: (which, r, jnp.where(_any_eq(b, last_blocks), b * per, (b + 1) * per), 0)

    blk = (None, None, A_BLOCK, A_WIDTH)
    halo = (None, None, A_HALF, A_WIDTH)
    win = A_SUB + 2 * A_HALF
    in_specs = [pl.BlockSpec(blk, cur(0)),
                pl.BlockSpec(halo, prev(1)), pl.BlockSpec(blk, cur(1)), pl.BlockSpec(halo, nxt(1)),
                pl.BlockSpec(halo, prev(2)), pl.BlockSpec(blk, cur(2)), pl.BlockSpec(halo, nxt(2)),
                pl.BlockSpec((A_HEADS, A_SUB, win), lambda r, b: (0, 0, 0))]
    out_idx = lambda r, b: (r, b, 0)
    out_specs = [pl.BlockSpec((None, A_BLOCK, A_WIDTH), out_idx), pl.BlockSpec((None, A_BLOCK, V7X_LANES), out_idx)]
    out_shape = [jax.ShapeDtypeStruct((d, L, A_WIDTH), F32), jax.ShapeDtypeStruct((d, L, V7X_LANES), F32)]
    vmem = 7 * 2 * A_BLOCK * A_WIDTH * 2 + 2 * A_HEADS * A_SUB * win * 4 + 2 * A_BLOCK * A_WIDTH * 4 + (8 << 20)
    return pl.pallas_call(
        functools.partial(_attn_a_kernel, first_blocks=first_blocks, last_blocks=last_blocks),
        grid=(d, nb), in_specs=in_specs, out_specs=out_specs, out_shape=out_shape,
        compiler_params=_params(2, vmem), name=f"attn_a_d{d}",
    )(a, a, a, a, a, a, a, bias)


def _attn_b_kernel(qi_ref, ki_ref, hi_ref, fl_ref, q_ref, k_ref, v_ref, o_ref, m_sc, l_sc, acc_sc):
    step = pl.program_id(0)
    flags = fl_ref[step]

    @pl.when((flags & 1) != 0)
    def _():
        m_sc[...] = jnp.full(m_sc.shape, NEG_INF, F32)
        l_sc[...] = jnp.zeros(l_sc.shape, F32)
        acc_sc[...] = jnp.zeros(acc_sc.shape, F32)


    tq = q_ref.shape[0]
    q = jnp.concatenate([q_ref[:, g * HEAD_DIM:(g + 1) * HEAD_DIM] for g in range(B_GROUPS)], axis=0)

    n_chunks = k_ref.shape[0] // B_CK

    def chunk_rows(ci):
        return pl.ds(pl.multiple_of(ci * B_CK, B_CK), B_CK)

    def scores(ci):
        return lax.dot_general(q, k_ref[chunk_rows(ci), :], (((1,), (1,)), ((), ())),
                               preferred_element_type=F32)

    def softmax_pv(ci, t):
        v = v_ref[chunk_rows(ci), :]
        m_prev = m_sc[...]
        m_new = jnp.maximum(m_prev, jnp.max(t, axis=1, keepdims=True))
        alpha = jnp.exp2(m_prev - m_new)
        p = jnp.exp2(t - jnp.concatenate([m_new] * (B_CK // V7X_LANES), axis=1))
        l_sc[...] = alpha * l_sc[...] + jnp.sum(p, axis=1, keepdims=True)
        acc_sc[...] = alpha * acc_sc[...] + jnp.dot(p.astype(v.dtype), v, preferred_element_type=F32)
        m_sc[...] = m_new

    def chunk(ci, carry):
        softmax_pv(ci, scores(ci))
        return carry

    lax.fori_loop(0, n_chunks, chunk, 0)

    @pl.when((flags & 2) != 0)
    def _():
        for g in range(B_GROUPS):
            rows = slice(g * tq, (g + 1) * tq)
            o_ref[:, g * HEAD_DIM:(g + 1) * HEAD_DIM] = (acc_sc[rows, :] / l_sc[rows, :]).astype(o_ref.dtype)


def _attn_b_schedule(seqs, tk):
    qi, ki, hi, fl = [], [], [], []
    for s0, sl in seqs:
        nkv = sl // tk
        for h in range(B_KV_HEADS):
            for qb in range(sl // B_TQ):
                for kb in range(nkv):
                    qi.append(s0 // B_TQ + qb)
                    ki.append(s0 // tk + kb)
                    hi.append(h)
                    fl.append((1 if kb == 0 else 0) | (2 if kb == nkv - 1 else 0))
    return [np.asarray(a, np.int32) for a in (qi, ki, hi, fl)]


def _attn_b(qb, kb, vb, seqs):
    T = qb.shape[0]
    tk = min([B_TK] + [sl for _, sl in seqs])
    assert all(sl % tk == 0 and s0 % tk == 0 and sl % B_TQ == 0 for s0, sl in seqs)
    qi, ki, hi, fl = _attn_b_schedule(seqs, tk)
    gw = B_GROUPS * HEAD_DIM
    rows = B_GROUPS * B_TQ
    grid_spec = pltpu.PrefetchScalarGridSpec(
        num_scalar_prefetch=4, grid=(len(qi),),
        in_specs=[pl.BlockSpec((B_TQ, gw), lambda s, qi, ki, hi, fl: (qi[s], hi[s])),
                  pl.BlockSpec((tk, HEAD_DIM), lambda s, qi, ki, hi, fl: (ki[s], hi[s])),
                  pl.BlockSpec((tk, HEAD_DIM), lambda s, qi, ki, hi, fl: (ki[s], hi[s]))],
        out_specs=pl.BlockSpec((B_TQ, gw), lambda s, qi, ki, hi, fl: (qi[s], hi[s])),
        scratch_shapes=[pltpu.VMEM((rows, V7X_LANES), F32)] * 3)
    vmem = 4 * B_TQ * gw * 2 + 4 * tk * HEAD_DIM * 2 + 3 * rows * V7X_LANES * 4 + 4 * rows * B_CK * 4 + (8 << 20)
    return pl.pallas_call(
        _attn_b_kernel, grid_spec=grid_spec, out_shape=jax.ShapeDtypeStruct((T, B_WIDTH), BF16),
        compiler_params=_params(1, vmem), name="attn_b",
    )(jnp.asarray(qi), jnp.asarray(ki), jnp.asarray(hi), jnp.asarray(fl), qb, kb, vb)


def _branch_kernel(*refs, with_router):
    (o1_ref, o4_ref, o16_ref, l1_ref, l4_ref, l16_ref, ob_ref, sga_ref, sgb_ref, x_ref,
     wba_ref, wbb_ref, wo_ref, g_ref, b_ref) = refs[:15]
    rest = refs[15:]
    if with_router:
        rw_ref, x1_ref, x1c_ref, route_ref, o4_sc, o16_sc, l4_sc, l16_sc = rest
    else:
        x1_ref, x1b_ref, o4_sc, o16_sc, l4_sc, l16_sc = rest
    tm = x_ref.shape[0]
    for d, src, dst, lsrc, ldst in ((4, o4_ref, o4_sc, l4_ref, l4_sc), (16, o16_ref, o16_sc, l16_ref, l16_sc)):
        for r in range(d):
            ldst[pl.ds(r, tm // d, stride=d), :] = lsrc[r]
            for h in range(A_HEADS):
                dst[h, pl.ds(r, tm // d, stride=d), :] = src[r, :, h * HEAD_DIM:(h + 1) * HEAD_DIM]
    l1, l4, l16 = l1_ref[0], l4_sc[...], l16_sc[...]
    mx = jnp.maximum(jnp.maximum(l1, l4), l16)
    e1, e4, e16 = jnp.exp(l1 - mx), jnp.exp(l4 - mx), jnp.exp(l16 - mx)
    den = e1 + e4 + e16
    w1, w4, w16 = e1 / den, e4 / den, e16 / den
    parts = []
    for h in range(A_HEADS):
        hs = slice(h * HEAD_DIM, (h + 1) * HEAD_DIM)
        oa_h = w1[:, h:h + 1] * o1_ref[0, :, hs] + w4[:, h:h + 1] * o4_sc[h] + w16[:, h:h + 1] * o16_sc[h]
        parts.append(oa_h.astype(BF16))
    oa = jnp.concatenate(parts, axis=1)
    ya = jnp.dot(oa, wba_ref[...], preferred_element_type=F32)
    yb = jnp.dot(ob_ref[...], wbb_ref[...], preferred_element_type=F32)
    merged = sga_ref[...].astype(F32) * ya + sgb_ref[...].astype(F32) * yb
    y = jnp.dot(merged.astype(BF16), wo_ref[...], preferred_element_type=F32)
    out = _layer_norm_rows(DEEPNORM_ALPHA * x_ref[...] + y, g_ref[...], b_ref[...])
    x1_ref[...] = out
    if not with_router:
        x1b_ref[...] = out.astype(BF16)
    if with_router:
        for c in range(ROW_SLAB):
            x1c_ref[pl.ds(c, tm, stride=ROW_SLAB), :] = out[:, c * V7X_LANES:(c + 1) * V7X_LANES]
        out_hi = out.astype(BF16)
        out_lo = (out - out_hi.astype(F32)).astype(BF16)
        both = jnp.dot(out_hi, rw_ref[...], preferred_element_type=F32)
        logits = (both[:, :V7X_LANES] + both[:, V7X_LANES:]
                  + jnp.dot(out_lo, rw_ref[:, :V7X_LANES], preferred_element_type=F32))
        lane = lax.broadcasted_iota(jnp.int32, logits.shape, 1)
        logits = jnp.where(lane < N_EXPERTS, logits, -jnp.inf)
        v1 = jnp.max(logits, axis=1, keepdims=True)
        i1 = jnp.min(jnp.where(logits == v1, lane, V7X_LANES), axis=1, keepdims=True)
        rem = jnp.where(lane == i1, -jnp.inf, logits)
        v2 = jnp.max(rem, axis=1, keepdims=True)
        i2 = jnp.min(jnp.where(rem == v2, lane, V7X_LANES), axis=1, keepdims=True)
        e2 = jnp.exp(v2 - v1)
        g1 = 1.0 / (1.0 + e2)
        g2 = e2 / (1.0 + e2)
        route = jnp.where(lane == 0, i1.astype(F32),
                          jnp.where(lane == 1, i2.astype(F32),
                                    jnp.where(lane == 2, g1, jnp.where(lane == 3, g2, 0.0))))
        route_ref[...] = route


def _branch(o1, o4, o16, l1, l4, l16, ob, gates, x, wba, wbb, wo, g, b, router_w=None):
    T = x.shape[0]
    tm = BRANCH_TM
    with_router = router_w is not None
    row = lambda m: (m, 0)
    const = lambda m: (0, 0)
    single = pl.Buffered(1)
    in_specs = [
        pl.BlockSpec((1, tm, A_WIDTH), lambda m: (0, m, 0)),
        pl.BlockSpec((4, tm // 4, A_WIDTH), lambda m: (0, m, 0)),
        pl.BlockSpec((16, tm // 16, A_WIDTH), lambda m: (0, m, 0)),
        pl.BlockSpec((1, tm, V7X_LANES), lambda m: (0, m, 0)),
        pl.BlockSpec((4, tm // 4, V7X_LANES), lambda m: (0, m, 0)),
        pl.BlockSpec((16, tm // 16, V7X_LANES), lambda m: (0, m, 0)),
        pl.BlockSpec((tm, B_WIDTH), row),
        pl.BlockSpec((tm, D_MODEL), lambda m: (m, 0)),
        pl.BlockSpec((tm, D_MODEL), lambda m: (m, 1)),
        pl.BlockSpec((tm, D_MODEL), row),
        pl.BlockSpec((A_WIDTH, D_MODEL), const, pipeline_mode=single),
        pl.BlockSpec((B_WIDTH, D_MODEL), const, pipeline_mode=single),
        pl.BlockSpec((D_MODEL, D_MODEL), const, pipeline_mode=single),
        pl.BlockSpec((1, D_MODEL), const),
        pl.BlockSpec((1, D_MODEL), const),
    ]
    args = [o1, o4, o16, l1, l4, l16, ob, gates, gates, x, wba, wbb, wo, g, b]
    out_specs = [pl.BlockSpec((tm, D_MODEL), row)]
    out_shape = [jax.ShapeDtypeStruct((T, D_MODEL), F32)]
    if with_router:
        in_specs.append(pl.BlockSpec((D_MODEL, 2 * V7X_LANES), const, pipeline_mode=single))
        args.append(router_w)
        out_specs += [pl.BlockSpec((tm * ROW_SLAB, V7X_LANES), row), pl.BlockSpec((tm, V7X_LANES), row)]
        out_shape += [jax.ShapeDtypeStruct((T * ROW_SLAB, V7X_LANES), F32),
                      jax.ShapeDtypeStruct((T, V7X_LANES), F32)]
    else:
        out_specs.append(pl.BlockSpec((tm, D_MODEL), row))
        out_shape.append(jax.ShapeDtypeStruct((T, D_MODEL), BF16))
    scratch = [pltpu.VMEM((A_HEADS, tm, HEAD_DIM), F32), pltpu.VMEM((A_HEADS, tm, HEAD_DIM), F32),
               pltpu.VMEM((tm, V7X_LANES), F32), pltpu.VMEM((tm, V7X_LANES), F32)]
    vmem = ((A_WIDTH + B_WIDTH + D_MODEL) * D_MODEL * 2 + D_MODEL * V7X_LANES * 4
            + 2 * 3 * tm * A_WIDTH * 4 + 2 * tm * A_WIDTH * 4 + 2 * tm * B_WIDTH * 2 + 4 * tm * D_MODEL * 2
            + 2 * tm * D_MODEL * 4 + 2 * tm * D_MODEL * 6 + 8 * tm * D_MODEL * 4 + (8 << 20))
    return pl.pallas_call(
        functools.partial(_branch_kernel, with_router=with_router),
        grid=(T // tm,), in_specs=in_specs, out_specs=out_specs, out_shape=out_shape,
        scratch_shapes=scratch, compiler_params=_params(1, vmem),
        name="branch_router" if with_router else "branch",
    )(*args)


def _swiglu_accumulate(xb, wg_ref, wu_ref, wd_ref, acc_ref):
    g = jnp.dot(xb, wg_ref[...], preferred_element_type=F32)
    u = jnp.dot(xb, wu_ref[...], preferred_element_type=F32)
    a = (g * jax.nn.sigmoid(g) * u).astype(BF16)
    acc_ref[...] += jnp.dot(a, wd_ref[...], preferred_element_type=F32)


def _ffn_dense_kernel(xb_ref, x_ref, wg_ref, wu_ref, wd_ref, g_ref, b_ref, o_ref, ob_ref):
    j = pl.program_id(1)

    @pl.when(j == 0)
    def _():
        o_ref[...] = jnp.zeros(o_ref.shape, F32)

    _swiglu_accumulate(xb_ref[...], wg_ref, wu_ref, wd_ref, o_ref)

    @pl.when(j == pl.num_programs(1) - 1)
    def _():
        out = _layer_norm_rows(DEEPNORM_ALPHA * x_ref[...] + o_ref[...], g_ref[...], b_ref[...])
        o_ref[...] = out
        ob_ref[...] = out.astype(BF16)


def _ffn_dense(xb, x, wg, wu, wd, g, b):
    T = x.shape[0]
    tm, tf = FFN_TM, FFN_TF
    row = lambda m, j: (m, 0)
    const = lambda m, j: (0, 0)
    in_specs = [pl.BlockSpec((tm, D_MODEL), row), pl.BlockSpec((tm, D_MODEL), row),
                pl.BlockSpec((D_MODEL, tf), lambda m, j: (0, j)),
                pl.BlockSpec((D_MODEL, tf), lambda m, j: (0, j)),
                pl.BlockSpec((tf, D_MODEL), lambda m, j: (j, 0)),
                pl.BlockSpec((1, D_MODEL), const), pl.BlockSpec((1, D_MODEL), const)]
    out_specs = [pl.BlockSpec((tm, D_MODEL), row), pl.BlockSpec((tm, D_MODEL), row)]
    out_shape = [jax.ShapeDtypeStruct((T, D_MODEL), F32), jax.ShapeDtypeStruct((T, D_MODEL), BF16)]
    vmem = (2 * tm * D_MODEL * 2 + 2 * tm * D_MODEL * 4 + 2 * 3 * D_MODEL * tf * 2 + 2 * tm * D_MODEL * 6
            + 4 * tm * tf * 4 + 2 * tm * D_MODEL * 4 + (8 << 20))
    return pl.pallas_call(
        _ffn_dense_kernel, grid=(T // tm, D_FF // tf), in_specs=in_specs, out_specs=out_specs,
        out_shape=out_shape, compiler_params=_params(2, vmem), name="ffn_dense",
    )(xb, x, wg, wu, wd, g, b)


def _moe_gather_kernel(src_ref, x_hbm, o_ref, slab_sc, sem):
    i = pl.program_id(0)
    n_tiles = pl.num_programs(0) - 1
    tm = src_ref.shape[2]
    slot = lax.rem(i, 2)

    def slab_copy(s, j, t):
        return pltpu.make_async_copy(x_hbm.at[pl.ds(t * ROW_SLAB, ROW_SLAB), :],
                                     slab_sc.at[s, pl.ds(j * ROW_SLAB, ROW_SLAB), :], sem.at[s])

    @pl.when(i < n_tiles)
    def _():
        def issue(j, c):
            slab_copy(slot, j, src_ref[0, 0, j]).start()
            return c
        lax.fori_loop(0, tm, issue, 0, unroll=DMA_UNROLL)

    @pl.when(i > 0)
    def _():
        done = 1 - slot

        def wait(j, c):
            slab_copy(done, j, 0).wait()
            return c
        lax.fori_loop(0, tm, wait, 0, unroll=DMA_UNROLL)
        for c in range(ROW_SLAB):
            cs = slice(c * V7X_LANES, (c + 1) * V7X_LANES)
            o_ref[:, cs] = slab_sc[done, pl.ds(c, tm, stride=ROW_SLAB), :].astype(o_ref.dtype)


def _moe_gather(xc, src):
    n_tiles, _, tm = src.shape
    vmem = 2 * tm * D_MODEL * 4 + 2 * tm * D_MODEL * 2 + 2 * tm * D_MODEL * 4 + (8 << 20)
    return pl.pallas_call(
        _moe_gather_kernel, grid=(n_tiles + 1,),
        in_specs=[pl.BlockSpec((1, 1, tm), lambda i: (jnp.minimum(i, n_tiles - 1), 0, 0), memory_space=pltpu.SMEM),
                  pl.BlockSpec(memory_space=pl.ANY)],
        out_specs=pl.BlockSpec((tm, D_MODEL), lambda i: (jnp.maximum(i - 1, 0), 0)),
        out_shape=jax.ShapeDtypeStruct((n_tiles * tm, D_MODEL), BF16),
        scratch_shapes=[pltpu.VMEM((2, tm * ROW_SLAB, V7X_LANES), F32), pltpu.SemaphoreType.DMA((2,))],
        compiler_params=_params(1, vmem), name="moe_gather",
    )(src, xc)


def _moe_ffn_kernel(te_ref, tv_ref, x_ref, wg_ref, wu_ref, wd_ref, o_ref):
    i = pl.program_id(0)
    j = pl.program_id(1)

    @pl.when(j == 0)
    def _():
        o_ref[...] = jnp.zeros(o_ref.shape, F32)

    @pl.when(tv_ref[i] != 0)
    def _():
        _swiglu_accumulate(x_ref[...], wg_ref, wu_ref, wd_ref, o_ref)


def _moe_ffn(xs, tile_expert, tile_valid, wg, wu, wd):
    P = xs.shape[0]
    tm, tf = MOE_TM, MOE_TF
    nj = D_FF_EXPERT // tf

    def jeff(i, j, tv):
        return jnp.where(tv[i] != 0, j, nj - 1)

    grid_spec = pltpu.PrefetchScalarGridSpec(
        num_scalar_prefetch=2, grid=(P // tm, nj),
        in_specs=[pl.BlockSpec((tm, D_MODEL), lambda i, j, te, tv: (i, 0)),
                  pl.BlockSpec((None, D_MODEL, tf), lambda i, j, te, tv: (te[i], 0, jeff(i, j, tv))),
                  pl.BlockSpec((None, D_MODEL, tf), lambda i, j, te, tv: (te[i], 0, jeff(i, j, tv))),
                  pl.BlockSpec((None, tf, D_MODEL), lambda i, j, te, tv: (te[i], jeff(i, j, tv), 0))],
        out_specs=pl.BlockSpec((tm, D_MODEL), lambda i, j, te, tv: (i, 0)))
    vmem = (2 * tm * D_MODEL * 2 + 2 * 3 * D_MODEL * tf * 2 + 2 * tm * D_MODEL * 4
            + 4 * tm * tf * 4 + 2 * tm * D_MODEL * 4 + (8 << 20))
    return pl.pallas_call(
        _moe_ffn_kernel, grid_spec=grid_spec, out_shape=jax.ShapeDtypeStruct((P, D_MODEL), F32),
        compiler_params=_params(2, vmem), name="moe_ffn",
    )(tile_expert, tile_valid, xs, wg, wu, wd)


def _moe_combine_kernel(dest_ref, y_hbm, route_ref, x_ref, g_ref, b_ref, o_ref, ob_ref, y0_sc, y1_sc, sem,
                        *, split_tiles):
    tm = x_ref.shape[0]

    def row_copy(j, k, p):
        dst = y0_sc if k == 0 else y1_sc
        return pltpu.make_async_copy(y_hbm.at[pl.ds(p, 1), :], dst.at[pl.ds(j, 1), :], sem)

    def issue(j, c):
        row_copy(j, 0, dest_ref[0, 0, 2 * j]).start()
        row_copy(j, 1, dest_ref[0, 0, 2 * j + 1]).start()
        return c

    def wait(j, c):
        row_copy(j, 0, 0).wait()
        row_copy(j, 1, 0).wait()
        return c

    lax.fori_loop(0, tm, issue, 0, unroll=DMA_UNROLL)
    lax.fori_loop(0, tm, wait, 0, unroll=DMA_UNROLL)
    route = route_ref[...]
    f = route[:, 2:3] * y0_sc[...] + route[:, 3:4] * y1_sc[...]
    out = _layer_norm_rows(DEEPNORM_ALPHA * x_ref[...] + f, g_ref[...], b_ref[...])
    if split_tiles is None:
        o_ref[...] = out
        ob_ref[...] = out.astype(BF16)
    else:
        m = pl.program_id(0)

        @pl.when(m < split_tiles)
        def _():
            o_ref[...] = out

        @pl.when(m >= split_tiles)
        def _():
            ob_ref[...] = out


def _moe_combine(y, dest, route, x, g, b, split_rows=None):
    T = x.shape[0]
    tm = COMBINE_TM
    row = lambda m: (m, 0)
    const = lambda m: (0, 0)
    if split_rows is None:
        split_tiles = None
        out_specs = [pl.BlockSpec((tm, D_MODEL), row), pl.BlockSpec((tm, D_MODEL), row)]
        out_shape = [jax.ShapeDtypeStruct((T, D_MODEL), F32), jax.ShapeDtypeStruct((T, D_MODEL), BF16)]
    else:
        assert split_rows % tm == 0 and 0 < split_rows < T
        split_tiles = split_rows // tm
        out_specs = [pl.BlockSpec((tm, D_MODEL), lambda m: (jnp.minimum(m, split_tiles - 1), 0)),
                     pl.BlockSpec((tm, D_MODEL), lambda m: (jnp.maximum(m - split_tiles, 0), 0))]
        out_shape = [jax.ShapeDtypeStruct((split_rows, D_MODEL), F32),
                     jax.ShapeDtypeStruct((T - split_rows, D_MODEL), F32)]
    vmem = 2 * tm * D_MODEL * 4 + 2 * tm * D_MODEL * 4 + 2 * tm * D_MODEL * 6 + 6 * tm * D_MODEL * 4 + (8 << 20)
    return pl.pallas_call(
        functools.partial(_moe_combine_kernel, split_tiles=split_tiles), grid=(T // tm,),
        in_specs=[pl.BlockSpec((1, 1, 2 * tm), lambda m: (m, 0, 0), memory_space=pltpu.SMEM),
                  pl.BlockSpec(memory_space=pl.ANY),
                  pl.BlockSpec((tm, V7X_LANES), row),
                  pl.BlockSpec((tm, D_MODEL), row),
                  pl.BlockSpec((1, D_MODEL), const), pl.BlockSpec((1, D_MODEL), const)],
        out_specs=out_specs, out_shape=out_shape,
        scratch_shapes=[pltpu.VMEM((tm, D_MODEL), F32), pltpu.VMEM((tm, D_MODEL), F32),
                        pltpu.SemaphoreType.DMA(())],
        compiler_params=_params(1, vmem), name="moe_combine",
    )(dest.reshape(T // tm, 1, 2 * tm), y, route, x, g, b)


def _route_plan(route, tm):
    T = route.shape[0]
    n_assign = T * TOP_K
    n_tiles = n_assign // tm + N_EXPERTS
    flat_e = route[:, :TOP_K].astype(jnp.int32).reshape(n_assign)
    onehot = (jnp.arange(N_EXPERTS, dtype=jnp.int32)[:, None] == flat_e[None, :]).astype(F32)
    blocks = onehot.reshape(N_EXPERTS, n_assign // V7X_LANES, V7X_LANES)
    tri = (jnp.arange(V7X_LANES)[:, None] >= jnp.arange(V7X_LANES)[None, :]).astype(F32)
    within = jnp.einsum("ebj,ij->ebi", blocks, tri)
    totals = within[:, :, -1]
    running = jnp.cumsum(totals, axis=1)
    csum = (within + (running - totals)[:, :, None]).reshape(N_EXPERTS, n_assign)
    counts = running[:, -1].astype(jnp.int32)
    tiles_per = (counts + tm - 1) // tm
    tile_end = jnp.cumsum(tiles_per)
    row_start = (tile_end - tiles_per) * tm
    dest = jnp.sum(onehot * (csum - 1.0 + row_start.astype(F32)[:, None]), axis=0).astype(jnp.int32)
    tile_ids = jnp.arange(n_tiles, dtype=jnp.int32)
    n_valid = tile_end[-1]
    tile_valid = (tile_ids < n_valid).astype(jnp.int32)
    owner = jnp.sum((jnp.minimum(tile_ids, n_valid - 1)[:, None] >= tile_end[None, :]).astype(jnp.int32), axis=1)
    tile_expert = jnp.minimum(owner, N_EXPERTS - 1).astype(jnp.int32)
    src = jnp.zeros((n_tiles * tm,), jnp.int32).at[dest].set(jnp.arange(n_assign, dtype=jnp.int32) // TOP_K)
    return dest.reshape(T, TOP_K), src.reshape(n_tiles, 1, tm), tile_expert, tile_valid


def _t5_bucket(rel):
    nb = N_BUCKETS // 2
    max_exact = nb // 2
    n = jnp.abs(rel)
    large = max_exact + (jnp.log(jnp.maximum(n, 1).astype(F32) / max_exact)
                         / math.log(MAX_DISTANCE / max_exact) * (nb - max_exact)).astype(jnp.int32)
    large = jnp.minimum(large, nb - 1)
    return jnp.where(rel > 0, nb, 0) + jnp.where(n < max_exact, n, large)


def _a_bias_table(rel_bias, d):
    i = jnp.arange(A_SUB, dtype=jnp.int32)[:, None]
    j = jnp.arange(A_SUB + 2 * A_HALF, dtype=jnp.int32)[None, :]
    rel = j - A_HALF - i
    bucket = _t5_bucket(rel * d)
    hit = bucket[None, :, :, None] == jnp.arange(N_BUCKETS, dtype=jnp.int32)
    bias = jnp.sum(jnp.where(hit, rel_bias.astype(F32).T[:, None, None, :], 0.0), axis=-1)
    return jnp.where((jnp.abs(rel) <= A_HALF)[None], bias, NEG_INF)


def _rope_tables(seqs):
    pos = jnp.concatenate([jnp.arange(sl, dtype=jnp.int32) for _, sl in seqs])
    row = (pos // GRID_W).astype(F32)
    col = (pos % GRID_W).astype(F32)
    n_freq = HEAD_DIM // 4
    inv_freq = ROPE_THETA ** (-jnp.arange(n_freq, dtype=F32) / n_freq)
    ang_r = row[:, None] * inv_freq[None, :]
    ang_c = col[:, None] * inv_freq[None, :]
    ang = jnp.concatenate([ang_r, ang_r, ang_c, ang_c], axis=-1)
    sign = jnp.where((jnp.arange(HEAD_DIM) % (HEAD_DIM // 2)) < (HEAD_DIM // 4), -1.0, 1.0).astype(F32)
    return jnp.cos(ang), jnp.sin(ang) * sign[None, :]


def kernel(x_prompt, x_sample, w_in, b_gate, q_norm_g, k_norm_g, rel_bias, w_branch_a, w_branch_b, w_out,
           ln1_g, ln1_b, ln2_g, ln2_b, ffn_w_gate, ffn_w_up, ffn_w_down, router_w, exp_w_gate, exp_w_up,
           exp_w_down):
    seqs = []
    for xs in (x_prompt, x_sample):
        for _ in range(xs.shape[0]):
            seqs.append((sum(sl for _, sl in seqs), xs.shape[1]))
    seqs = tuple(seqs)
    T = sum(sl for _, sl in seqs)
    assert all(s0 % max(PROJ_TM, B_TQ) == 0 and sl % max(PROJ_TM, B_TQ) == 0 for s0, sl in seqs)

    x = jnp.concatenate([x_prompt.reshape(-1, D_MODEL), x_sample.reshape(-1, D_MODEL)], axis=0)
    xb = x.astype(BF16)
    cos, sin_signed = _rope_tables(seqs)
    biases = [_a_bias_table(rel_bias, d) for _, d in DILATED_PATTERNS]
    router_pad = jnp.pad(router_w.astype(F32), ((0, 0), (0, 0), (0, V7X_LANES - N_EXPERTS)))
    router_hi = router_pad.astype(BF16)
    router_lo = (router_pad - router_hi.astype(F32)).astype(BF16)
    router_split = jnp.concatenate([router_hi, router_lo], axis=-1)
    n_prompt = x_prompt.shape[0] * x_prompt.shape[1]

    for l in range(DEPTH):
        a1, a4, a16, qb, kb, vb, gates = _in_proj(
            xb, w_in[l].astype(BF16), b_gate[l].reshape(1, 2 * D_MODEL),
            q_norm_g[l].reshape(1, HEAD_DIM), k_norm_g[l].reshape(1, HEAD_DIM), cos, sin_signed)
        o1, l1 = _attn_a(a1, biases[0], 1, seqs)
        o4, l4 = _attn_a(a4, biases[1], 4, seqs)
        o16, l16 = _attn_a(a16, biases[2], 16, seqs)
        ob = _attn_b(qb, kb, vb, seqs)
        moe = l % 2 == 1
        res = _branch(o1, o4, o16, l1, l4, l16, ob, gates, x,
                      w_branch_a[l].astype(BF16), w_branch_b[l].astype(BF16), w_out[l].astype(BF16),
                      ln1_g[l].reshape(1, D_MODEL), ln1_b[l].reshape(1, D_MODEL),
                      router_split[l // 2] if moe else None)
        g2, b2 = ln2_g[l].reshape(1, D_MODEL), ln2_b[l].reshape(1, D_MODEL)
        j = l // 2
        if not moe:
            x1, x1b = res
            x, xb = _ffn_dense(x1b, x1, ffn_w_gate[j].astype(BF16), ffn_w_up[j].astype(BF16),
                               ffn_w_down[j].astype(BF16), g2, b2)
        else:
            x1, x1c, route = res
            dest, src, tile_expert, tile_valid = _route_plan(route, MOE_TM)
            xs = _moe_gather(x1c, src)
            ys = _moe_ffn(xs, tile_expert, tile_valid, exp_w_gate[j].astype(BF16), exp_w_up[j].astype(BF16),
                          exp_w_down[j].astype(BF16))
            if l == DEPTH - 1:
                y_prompt, y_sample = _moe_combine(ys, dest, route, x1, g2, b2, split_rows=n_prompt)
                return (y_prompt.reshape(x_prompt.shape), y_sample.reshape(x_sample.shape))
            x, xb = _moe_combine(ys, dest, route, x1, g2, b2)

    return (x[:n_prompt].reshape(x_prompt.shape), x[n_prompt:].reshape(x_sample.shape))
```

```python
import functools
import math

import numpy as np
import jax
import jax.numpy as jnp
from jax import lax
from jax.experimental import pallas as pl
from jax.experimental.pallas import tpu as pltpu

D_MODEL = 2048
DEPTH = 2
HEAD_DIM = 128
A_HEADS = 8
A_WIDTH = A_HEADS * HEAD_DIM
DILATED_PATTERNS = ((128, 1), (512, 4), (2048, 16))
A_HALF = 64
B_HEADS = 8
B_KV_HEADS = 2
B_GROUPS = B_HEADS // B_KV_HEADS
B_WIDTH = B_HEADS * HEAD_DIM
B_KV_WIDTH = B_KV_HEADS * HEAD_DIM
GRID_W = 64
ROPE_THETA = 10000.0
N_BUCKETS = 32
MAX_DISTANCE = 1024
IN_COLS = 3 * A_WIDTH + B_WIDTH + 2 * B_KV_WIDTH + 2 * D_MODEL
D_FF = 5632
N_EXPERTS = 8
TOP_K = 2
D_FF_EXPERT = 7168
DEEPNORM_ALPHA = (2.0 * DEPTH) ** 0.25
LN_EPS = 1e-5
RMS_EPS = 1e-6
NEG_INF = -1e30
ATTN_SCALE = HEAD_DIM ** -0.5

V7X_LANES = 128
V7X_VMEM_BYTES = 64 * 1024 * 1024
ROW_SLAB = D_MODEL // V7X_LANES

PROJ_TM = 1024
PROJ_TN = 512
A_BLOCK = 256
A_SUB = 128
B_TQ = 1024
B_TK = 4096
B_CK = 512
BRANCH_TM = 256
FFN_TM = 512
FFN_TF = 512
MOE_TM = 512
MOE_TF = 1024
COMBINE_TM = 256
DMA_UNROLL = 8

BF16 = jnp.bfloat16
F32 = jnp.float32

assert all(w // (2 * d) == A_HALF for w, d in DILATED_PATTERNS)
assert IN_COLS % PROJ_TN == 0 and D_FF % FFN_TF == 0 and D_FF_EXPERT % MOE_TF == 0


def _vmem_limit(nbytes):
    return int(min(nbytes, V7X_VMEM_BYTES - 4 * 1024 * 1024))


def _params(n_axes, vmem_bytes):
    return pltpu.CompilerParams(
        dimension_semantics=("arbitrary",) * n_axes, vmem_limit_bytes=_vmem_limit(vmem_bytes))


def _any_eq(b, values):
    return functools.reduce(jnp.logical_or, [b == v for v in values])


def _layer_norm_rows(z, g, b):
    mu = jnp.mean(z, axis=-1, keepdims=True)
    zc = z - mu
    var = jnp.mean(zc * zc, axis=-1, keepdims=True)
    return zc * lax.rsqrt(var + LN_EPS) * g + b


_N_QA, _N_KA, _N_VA = 0, A_WIDTH // PROJ_TN, 2 * A_WIDTH // PROJ_TN
_N_QB = 3 * A_WIDTH // PROJ_TN
_N_KVB = _N_QB + B_WIDTH // PROJ_TN
_N_GATE = _N_KVB + 1
_N_TILES = IN_COLS // PROJ_TN
assert 2 * B_KV_WIDTH == PROJ_TN and A_WIDTH % PROJ_TN == 0


def _rms_rope_store(acc, n_heads, g, cos, sin_signed, post_scale, out_ref):
    lane = lax.broadcasted_iota(jnp.int32, (acc.shape[0], HEAD_DIM), 1)
    low = (lane % (HEAD_DIM // 2)) < (HEAD_DIM // 4)
    for h in range(n_heads):
        xh = acc[:, h * HEAD_DIM:(h + 1) * HEAD_DIM]
        y = xh * lax.rsqrt(jnp.mean(xh * xh, axis=-1, keepdims=True) + RMS_EPS) * g
        rot = jnp.where(low, pltpu.roll(y, HEAD_DIM - HEAD_DIM // 4, 1), pltpu.roll(y, HEAD_DIM // 4, 1))
        roped = y * cos + rot * sin_signed
        if post_scale != 1.0:
            roped = roped * post_scale
        out_ref[:, h * HEAD_DIM:(h + 1) * HEAD_DIM] = roped.astype(out_ref.dtype)


B_Q_PRESCALE = ATTN_SCALE * math.log2(math.e)


def _in_proj_kernel(x_ref, w_ref, bg_ref, qg_ref, kg_ref, cos_ref, sin_ref,
                    a1_ref, a4_ref, a16_ref, qb_ref, kb_ref, vb_ref, gate_ref, acc_ref):
    n = pl.program_id(1)
    tm = x_ref.shape[0]

    def product():
        return jnp.dot(x_ref[...], w_ref[...], preferred_element_type=F32)

    @pl.when(n < _N_QB)
    def _():
        acc = product()
        a1_ref[0] = acc.astype(a1_ref.dtype)
        for c in range(acc_ref.shape[0]):
            cs = slice(c * V7X_LANES, (c + 1) * V7X_LANES)
            acc_ref[c] = acc[:, cs]
            for d, o in ((4, a4_ref), (16, a16_ref)):
                for r in range(d):
                    o[r, :, cs] = acc_ref[c, pl.ds(r, tm // d, stride=d), :].astype(o.dtype)

    @pl.when(jnp.logical_and(n >= _N_QB, n < _N_KVB))
    def _():
        _rms_rope_store(product(), PROJ_TN // HEAD_DIM, qg_ref[...], cos_ref[...], sin_ref[...], B_Q_PRESCALE,
                        qb_ref)

    @pl.when(n == _N_KVB)
    def _():
        acc = product()
        _rms_rope_store(acc[:, :B_KV_WIDTH], B_KV_HEADS, kg_ref[...], cos_ref[...], sin_ref[...], 1.0, kb_ref)
        vb_ref[...] = acc[:, B_KV_WIDTH:].astype(vb_ref.dtype)

    @pl.when(n >= _N_GATE)
    def _():
        gate_ref[...] = jax.nn.sigmoid(product() + bg_ref[...]).astype(gate_ref.dtype)


def _in_proj(xb, w, bg, qg, kg, cos, sin_signed):
    T = xb.shape[0]
    tm, tn = PROJ_TM, PROJ_TN
    grid = (T // tm, _N_TILES)
    a_tiles = A_WIDTH // tn

    def clampn(lo, cnt):
        return lambda m, n: (m, jnp.clip(n - lo, 0, cnt - 1))

    def dil_spec(d):
        def index(m, n):
            nc = jnp.minimum(n, 3 * a_tiles - 1)
            return (nc // a_tiles, 0, m, nc % a_tiles)
        return pl.BlockSpec((None, d, tm // d, tn), index)

    def dil_shape(d):
        return jax.ShapeDtypeStruct((3, d, T // d, A_WIDTH), BF16)

    in_specs = [
        pl.BlockSpec((tm, D_MODEL), lambda m, n: (m, 0)),
        pl.BlockSpec((D_MODEL, tn), lambda m, n: (0, n)),
        pl.BlockSpec((1, tn), lambda m, n: (0, jnp.clip(n - _N_GATE, 0, 2 * D_MODEL // tn - 1))),
        pl.BlockSpec((1, HEAD_DIM), lambda m, n: (0, 0)),
        pl.BlockSpec((1, HEAD_DIM), lambda m, n: (0, 0)),
        pl.BlockSpec((tm, HEAD_DIM), lambda m, n: (m, 0)),
        pl.BlockSpec((tm, HEAD_DIM), lambda m, n: (m, 0)),
    ]
    out_specs = [dil_spec(1), dil_spec(4), dil_spec(16),
                 pl.BlockSpec((tm, tn), clampn(_N_QB, B_WIDTH // tn)),
                 pl.BlockSpec((tm, B_KV_WIDTH), lambda m, n: (m, 0)),
                 pl.BlockSpec((tm, B_KV_WIDTH), lambda m, n: (m, 0)),
                 pl.BlockSpec((tm, tn), clampn(_N_GATE, 2 * D_MODEL // tn))]
    out_shape = [dil_shape(1), dil_shape(4), dil_shape(16),
                 jax.ShapeDtypeStruct((T, B_WIDTH), BF16),
                 jax.ShapeDtypeStruct((T, B_KV_WIDTH), BF16),
                 jax.ShapeDtypeStruct((T, B_KV_WIDTH), BF16),
                 jax.ShapeDtypeStruct((T, 2 * D_MODEL), BF16)]
    vmem = (2 * tm * D_MODEL * 2 + 2 * D_MODEL * tn * 2 + 7 * 2 * tm * tn * 2 + 4 * tm * HEAD_DIM * 4
            + 6 * tm * tn * 4 + (8 << 20))
    return pl.pallas_call(
        _in_proj_kernel, grid=grid, in_specs=in_specs, out_specs=out_specs, out_shape=out_shape,
        scratch_shapes=[pltpu.VMEM((tn // V7X_LANES, tm, V7X_LANES), F32)],
        compiler_params=_params(2, vmem), name="in_proj",
    )(xb, w, bg, qg, kg, cos, sin_signed)


def _attn_a_kernel(q_ref, kp_ref, kc_ref, kn_ref, vp_ref, vc_ref, vn_ref, bias_ref, o_ref, lse_ref,
                   *, first_blocks, last_blocks):
    b = pl.program_id(1)
    win = A_SUB + 2 * A_HALF
    is_first = _any_eq(b, first_blocks)
    is_last = _any_eq(b, last_blocks)
    col = lax.broadcasted_iota(jnp.int32, (A_SUB, win), 1)
    lane = lax.broadcasted_iota(jnp.int32, (A_SUB, V7X_LANES), 1)
    n_sub = A_BLOCK // A_SUB

    def window(prev_ref, cur_ref, next_ref, start, hs):
        parts = []
        if start < 0:
            parts.append(prev_ref[A_HALF + start:, hs])
        parts.append(cur_ref[max(start, 0):min(start + win, A_BLOCK), hs])
        if start + win > A_BLOCK:
            parts.append(next_ref[:start + win - A_BLOCK, hs])
        return parts[0] if len(parts) == 1 else jnp.concatenate(parts, axis=0)

    for sb in range(n_sub):
        start = sb * A_SUB - A_HALF
        rows = slice(sb * A_SUB, (sb + 1) * A_SUB)
        lo = jnp.where(is_first, -start, 0) if start < 0 else 0
        hi = jnp.where(is_last, A_BLOCK - start, win) if start + win > A_BLOCK else win
        outside = jnp.logical_or(col < lo, col >= hi)
        lse_tile = jnp.zeros((A_SUB, V7X_LANES), F32)
        for h in range(A_HEADS):
            hs = slice(h * HEAD_DIM, (h + 1) * HEAD_DIM)
            q = q_ref[rows, hs]
            k = window(kp_ref, kc_ref, kn_ref, start, hs)
            v = window(vp_ref, vc_ref, vn_ref, start, hs)
            s = lax.dot_general(q, k, (((1,), (1,)), ((), ())), preferred_element_type=F32)
            s = jnp.where(outside, NEG_INF, s * ATTN_SCALE + bias_ref[h])
            m = jnp.max(s, axis=1, keepdims=True)
            p = jnp.exp(s - m)
            l = jnp.sum(p, axis=1, keepdims=True)
            o = jnp.dot(p.astype(v.dtype), v, preferred_element_type=F32)
            o_ref[rows, hs] = o / l
            lse_tile = jnp.where(lane == h, m + jnp.log(l), lse_tile)
        lse_ref[rows, :] = lse_tile


def _attn_a(a, bias, d, seqs):
    _, _, L, _ = a.shape
    nb = L // A_BLOCK
    assert all(s0 % (d * A_BLOCK) == 0 and sl % (d * A_BLOCK) == 0 for s0, sl in seqs)
    first_blocks = tuple(s0 // d // A_BLOCK for s0, _ in seqs)
    last_blocks = tuple((s0 + sl) // d // A_BLOCK - 1 for s0, sl in seqs)

    def cur(which):
        return lambda r, b: (which, r, b, 0)

    per = A_BLOCK // A_HALF

    def prev(which):
        return lambda r, b: (which, r, jnp.where(_any_eq(b, first_blocks), b * per, b * per - 1), 0)

    def nxt(which):
        return lambda r, b: (which, r, jnp.where(_any_eq(b, last_blocks), b * per, (b + 1) * per), 0)

    blk = (None, None, A_BLOCK, A_WIDTH)
    halo = (None, None, A_HALF, A_WIDTH)
    win = A_SUB + 2 * A_HALF
    in_specs = [pl.BlockSpec(blk, cur(0)),
                pl.BlockSpec(halo, prev(1)), pl.BlockSpec(blk, cur(1)), pl.BlockSpec(halo, nxt(1)),
                pl.BlockSpec(halo, prev(2)), pl.BlockSpec(blk, cur(2)), pl.BlockSpec(halo, nxt(2)),
                pl.BlockSpec((A_HEADS, A_SUB, win), lambda r, b: (0, 0, 0))]
    out_idx = lambda r, b: (r, b, 0)
    out_specs = [pl.BlockSpec((None, A_BLOCK, A_WIDTH), out_idx), pl.BlockSpec((None, A_BLOCK, V7X_LANES), out_idx)]
    out_shape = [jax.ShapeDtypeStruct((d, L, A_WIDTH), F32), jax.ShapeDtypeStruct((d, L, V7X_LANES), F32)]
    vmem = 7 * 2 * A_BLOCK * A_WIDTH * 2 + 2 * A_HEADS * A_SUB * win * 4 + 2 * A_BLOCK * A_WIDTH * 4 + (8 << 20)
    return pl.pallas_call(
        functools.partial(_attn_a_kernel, first_blocks=first_blocks, last_blocks=last_blocks),
        grid=(d, nb), in_specs=in_specs, out_specs=out_specs, out_shape=out_shape,
        compiler_params=_params(2, vmem), name=f"attn_a_d{d}",
    )(a, a, a, a, a, a, a, bias)


def _attn_b_kernel(qi_ref, ki_ref, hi_ref, fl_ref, q_ref, k_ref, v_ref, o_ref, m_sc, l_sc, acc_sc):
    step = pl.program_id(0)
    flags = fl_ref[step]

    @pl.when((flags & 1) != 0)
    def _():
        m_sc[...] = jnp.full(m_sc.shape, NEG_INF, F32)
        l_sc[...] = jnp.zeros(l_sc.shape, F32)
        acc_sc[...] = jnp.zeros(acc_sc.shape, F32)


    tq = q_ref.shape[0]
    q = jnp.concatenate([q_ref[:, g * HEAD_DIM:(g + 1) * HEAD_DIM] for g in range(B_GROUPS)], axis=0)

    n_chunks = k_ref.shape[0] // B_CK

    def chunk_rows(ci):
        return pl.ds(pl.multiple_of(ci * B_CK, B_CK), B_CK)

    def scores(ci):
        return lax.dot_general(q, k_ref[chunk_rows(ci), :], (((1,), (1,)), ((), ())),
                               preferred_element_type=F32)

    def softmax_pv(ci, t):
        v = v_ref[chunk_rows(ci), :]
        m_prev = m_sc[...]
        m_new = jnp.maximum(m_prev, jnp.max(t, axis=1, keepdims=True))
        alpha = jnp.exp2(m_prev - m_new)
        p = jnp.exp2(t - jnp.concatenate([m_new] * (B_CK // V7X_LANES), axis=1))
        l_sc[...] = alpha * l_sc[...] + jnp.sum(p, axis=1, keepdims=True)
        acc_sc[...] = alpha * acc_sc[...] + jnp.dot(p.astype(v.dtype), v, preferred_element_type=F32)
        m_sc[...] = m_new

    def chunk(ci, carry):
        softmax_pv(ci, scores(ci))
        return carry

    lax.fori_loop(0, n_chunks, chunk, 0)

    @pl.when((flags & 2) != 0)
    def _():
        for g in range(B_GROUPS):
            rows = slice(g * tq, (g + 1) * tq)
            o_ref[:, g * HEAD_DIM:(g + 1) * HEAD_DIM] = (acc_sc[rows, :] / l_sc[rows, :]).astype(o_ref.dtype)


def _attn_b_schedule(seqs, tk):
    qi, ki, hi, fl = [], [], [], []
    for s0, sl in seqs:
        nkv = sl // tk
        for h in range(B_KV_HEADS):
            for qb in range(sl // B_TQ):
                for kb in range(nkv):
                    qi.append(s0 // B_TQ + qb)
                    ki.append(s0 // tk + kb)
                    hi.append(h)
                    fl.append((1 if kb == 0 else 0) | (2 if kb == nkv - 1 else 0))
    return [np.asarray(a, np.int32) for a in (qi, ki, hi, fl)]


def _attn_b(qb, kb, vb, seqs):
    T = qb.shape[0]
    tk = min([B_TK] + [sl for _, sl in seqs])
    assert all(sl % tk == 0 and s0 % tk == 0 and sl % B_TQ == 0 for s0, sl in seqs)
    qi, ki, hi, fl = _attn_b_schedule(seqs, tk)
    gw = B_GROUPS * HEAD_DIM
    rows = B_GROUPS * B_TQ
    grid_spec = pltpu.PrefetchScalarGridSpec(
        num_scalar_prefetch=4, grid=(len(qi),),
        in_specs=[pl.BlockSpec((B_TQ, gw), lambda s, qi, ki, hi, fl: (qi[s], hi[s])),
                  pl.BlockSpec((tk, HEAD_DIM), lambda s, qi, ki, hi, fl: (ki[s], hi[s])),
                  pl.BlockSpec((tk, HEAD_DIM), lambda s, qi, ki, hi, fl: (ki[s], hi[s]))],
        out_specs=pl.BlockSpec((B_TQ, gw), lambda s, qi, ki, hi, fl: (qi[s], hi[s])),
        scratch_shapes=[pltpu.VMEM((rows, V7X_LANES), F32)] * 3)
    vmem = 4 * B_TQ * gw * 2 + 4 * tk * HEAD_DIM * 2 + 3 * rows * V7X_LANES * 4 + 4 * rows * B_CK * 4 + (8 << 20)
    return pl.pallas_call(
        _attn_b_kernel, grid_spec=grid_spec, out_shape=jax.ShapeDtypeStruct((T, B_WIDTH), BF16),
        compiler_params=_params(1, vmem), name="attn_b",
    )(jnp.asarray(qi), jnp.asarray(ki), jnp.asarray(hi), jnp.asarray(fl), qb, kb, vb)


def _branch_kernel(*refs, with_router):
    (o1_ref, o4_ref, o16_ref, l1_ref, l4_ref, l16_ref, ob_ref, sga_ref, sgb_ref, x_ref,
     wba_ref, wbb_ref, wo_ref, g_ref, b_ref) = refs[:15]
    rest = refs[15:]
    if with_router:
        rw_ref, x1_ref, x1c_ref, route_ref, o4_sc, o16_sc, l4_sc, l16_sc = rest
    else:
        x1_ref, x1b_ref, o4_sc, o16_sc, l4_sc, l16_sc = rest
    tm = x_ref.shape[0]
    for d, src, dst, lsrc, ldst in ((4, o4_ref, o4_sc, l4_ref, l4_sc), (16, o16_ref, o16_sc, l16_ref, l16_sc)):
        for r in range(d):
            ldst[pl.ds(r, tm // d, stride=d), :] = lsrc[r]
            for h in range(A_HEADS):
                dst[h, pl.ds(r, tm // d, stride=d), :] = src[r, :, h * HEAD_DIM:(h + 1) * HEAD_DIM]
    l1, l4, l16 = l1_ref[0], l4_sc[...], l16_sc[...]
    mx = jnp.maximum(jnp.maximum(l1, l4), l16)
    e1, e4, e16 = jnp.exp(l1 - mx), jnp.exp(l4 - mx), jnp.exp(l16 - mx)
    den = e1 + e4 + e16
    w1, w4, w16 = e1 / den, e4 / den, e16 / den
    parts = []
    for h in range(A_HEADS):
        hs = slice(h * HEAD_DIM, (h + 1) * HEAD_DIM)
        oa_h = w1[:, h:h + 1] * o1_ref[0, :, hs] + w4[:, h:h + 1] * o4_sc[h] + w16[:, h:h + 1] * o16_sc[h]
        parts.append(oa_h.astype(BF16))
    oa = jnp.concatenate(parts, axis=1)
    ya = jnp.dot(oa, wba_ref[...], preferred_element_type=F32)
    yb = jnp.dot(ob_ref[...], wbb_ref[...], preferred_element_type=F32)
    merged = sga_ref[...].astype(F32) * ya + sgb_ref[...].astype(F32) * yb
    y = jnp.dot(merged.astype(BF16), wo_ref[...], preferred_element_type=F32)
    out = _layer_norm_rows(DEEPNORM_ALPHA * x_ref[...] + y, g_ref[...], b_ref[...])
    x1_ref[...] = out
    if not with_router:
        x1b_ref[...] = out.astype(BF16)
    if with_router:
        for c in range(ROW_SLAB):
            x1c_ref[pl.ds(c, tm, stride=ROW_SLAB), :] = out[:, c * V7X_LANES:(c + 1) * V7X_LANES]
        out_hi = out.astype(BF16)
        out_lo = (out - out_hi.astype(F32)).astype(BF16)
        both = jnp.dot(out_hi, rw_ref[...], preferred_element_type=F32)
        logits = (both[:, :V7X_LANES] + both[:, V7X_LANES:]
                  + jnp.dot(out_lo, rw_ref[:, :V7X_LANES], preferred_element_type=F32))
        lane = lax.broadcasted_iota(jnp.int32, logits.shape, 1)
        logits = jnp.where(lane < N_EXPERTS, logits, -jnp.inf)
        v1 = jnp.max(logits, axis=1, keepdims=True)
        i1 = jnp.min(jnp.where(logits == v1, lane, V7X_LANES), axis=1, keepdims=True)
        rem = jnp.where(lane == i1, -jnp.inf, logits)
        v2 = jnp.max(rem, axis=1, keepdims=True)
        i2 = jnp.min(jnp.where(rem == v2, lane, V7X_LANES), axis=1, keepdims=True)
        e2 = jnp.exp(v2 - v1)
        g1 = 1.0 / (1.0 + e2)
        g2 = e2 / (1.0 + e2)
        route = jnp.where(lane == 0, i1.astype(F32),
                          jnp.where(lane == 1, i2.astype(F32),
                                    jnp.where(lane == 2, g1, jnp.where(lane == 3, g2, 0.0))))
        route_ref[...] = route


def _branch(o1, o4, o16, l1, l4, l16, ob, gates, x, wba, wbb, wo, g, b, router_w=None):
    T = x.shape[0]
    tm = BRANCH_TM
    with_router = router_w is not None
    row = lambda m: (m, 0)
    const = lambda m: (0, 0)
    single = pl.Buffered(1)
    in_specs = [
        pl.BlockSpec((1, tm, A_WIDTH), lambda m: (0, m, 0)),
        pl.BlockSpec((4, tm // 4, A_WIDTH), lambda m: (0, m, 0)),
        pl.BlockSpec((16, tm // 16, A_WIDTH), lambda m: (0, m, 0)),
        pl.BlockSpec((1, tm, V7X_LANES), lambda m: (0, m, 0)),
        pl.BlockSpec((4, tm // 4, V7X_LANES), lambda m: (0, m, 0)),
        pl.BlockSpec((16, tm // 16, V7X_LANES), lambda m: (0, m, 0)),
        pl.BlockSpec((tm, B_WIDTH), row),
        pl.BlockSpec((tm, D_MODEL), lambda m: (m, 0)),
        pl.BlockSpec((tm, D_MODEL), lambda m: (m, 1)),
        pl.BlockSpec((tm, D_MODEL), row),
        pl.BlockSpec((A_WIDTH, D_MODEL), const, pipeline_mode=single),
        pl.BlockSpec((B_WIDTH, D_MODEL), const, pipeline_mode=single),
        pl.BlockSpec((D_MODEL, D_MODEL), const, pipeline_mode=single),
        pl.BlockSpec((1, D_MODEL), const),
        pl.BlockSpec((1, D_MODEL), const),
    ]
    args = [o1, o4, o16, l1, l4, l16, ob, gates, gates, x, wba, wbb, wo, g, b]
    out_specs = [pl.BlockSpec((tm, D_MODEL), row)]
    out_shape = [jax.ShapeDtypeStruct((T, D_MODEL), F32)]
    if with_router:
        in_specs.append(pl.BlockSpec((D_MODEL, 2 * V7X_LANES), const, pipeline_mode=single))
        args.append(router_w)
        out_specs += [pl.BlockSpec((tm * ROW_SLAB, V7X_LANES), row), pl.BlockSpec((tm, V7X_LANES), row)]
        out_shape += [jax.ShapeDtypeStruct((T * ROW_SLAB, V7X_LANES), F32),
                      jax.ShapeDtypeStruct((T, V7X_LANES), F32)]
    else:
        out_specs.append(pl.BlockSpec((tm, D_MODEL), row))
        out_shape.append(jax.ShapeDtypeStruct((T, D_MODEL), BF16))
    scratch = [pltpu.VMEM((A_HEADS, tm, HEAD_DIM), F32), pltpu.VMEM((A_HEADS, tm, HEAD_DIM), F32),
               pltpu.VMEM((tm, V7X_LANES), F32), pltpu.VMEM((tm, V7X_LANES), F32)]
    vmem = ((A_WIDTH + B_WIDTH + D_MODEL) * D_MODEL * 2 + D_MODEL * V7X_LANES * 4
            + 2 * 3 * tm * A_WIDTH * 4 + 2 * tm * A_WIDTH * 4 + 2 * tm * B_WIDTH * 2 + 4 * tm * D_MODEL * 2
            + 2 * tm * D_MODEL * 4 + 2 * tm * D_MODEL * 6 + 8 * tm * D_MODEL * 4 + (8 << 20))
    return pl.pallas_call(
        functools.partial(_branch_kernel, with_router=with_router),
        grid=(T // tm,), in_specs=in_specs, out_specs=out_specs, out_shape=out_shape,
        scratch_shapes=scratch, compiler_params=_params(1, vmem),
        name="branch_router" if with_router else "branch",
    )(*args)


def _swiglu_accumulate(xb, wg_ref, wu_ref, wd_ref, acc_ref):
    g = jnp.dot(xb, wg_ref[...], preferred_element_type=F32)
    u = jnp.dot(xb, wu_ref[...], preferred_element_type=F32)
    a = (g * jax.nn.sigmoid(g) * u).astype(BF16)
    acc_ref[...] += jnp.dot(a, wd_ref[...], preferred_element_type=F32)


def _ffn_dense_kernel(xb_ref, x_ref, wg_ref, wu_ref, wd_ref, g_ref, b_ref, o_ref, ob_ref):
    j = pl.program_id(1)

    @pl.when(j == 0)
    def _():
        o_ref[...] = jnp.zeros(o_ref.shape, F32)

    _swiglu_accumulate(xb_ref[...], wg_ref, wu_ref, wd_ref, o_ref)

    @pl.when(j == pl.num_programs(1) - 1)
    def _():
        out = _layer_norm_rows(DEEPNORM_ALPHA * x_ref[...] + o_ref[...], g_ref[...], b_ref[...])
        o_ref[...] = out
        ob_ref[...] = out.astype(BF16)


def _ffn_dense(xb, x, wg, wu, wd, g, b):
    T = x.shape[0]
    tm, tf = FFN_TM, FFN_TF
    row = lambda m, j: (m, 0)
    const = lambda m, j: (0, 0)
    in_specs = [pl.BlockSpec((tm, D_MODEL), row), pl.BlockSpec((tm, D_MODEL), row),
                pl.BlockSpec((D_MODEL, tf), lambda m, j: (0, j)),
                pl.BlockSpec((D_MODEL, tf), lambda m, j: (0, j)),
                pl.BlockSpec((tf, D_MODEL), lambda m, j: (j, 0)),
                pl.BlockSpec((1, D_MODEL), const), pl.BlockSpec((1, D_MODEL), const)]
    out_specs = [pl.BlockSpec((tm, D_MODEL), row), pl.BlockSpec((tm, D_MODEL), row)]
    out_shape = [jax.ShapeDtypeStruct((T, D_MODEL), F32), jax.ShapeDtypeStruct((T, D_MODEL), BF16)]
    vmem = (2 * tm * D_MODEL * 2 + 2 * tm * D_MODEL * 4 + 2 * 3 * D_MODEL * tf * 2 + 2 * tm * D_MODEL * 6
            + 4 * tm * tf * 4 + 2 * tm * D_MODEL * 4 + (8 << 20))
    return pl.pallas_call(
        _ffn_dense_kernel, grid=(T // tm, D_FF // tf), in_specs=in_specs, out_specs=out_specs,
        out_shape=out_shape, compiler_params=_params(2, vmem), name="ffn_dense",
    )(xb, x, wg, wu, wd, g, b)


def _moe_gather_kernel(src_ref, x_hbm, o_ref, slab_sc, sem):
    i = pl.program_id(0)
    n_tiles = pl.num_programs(0) - 1
    tm = src_ref.shape[2]
    slot = lax.rem(i, 2)

    def slab_copy(s, j, t):
        return pltpu.make_async_copy(x_hbm.at[pl.ds(t * ROW_SLAB, ROW_SLAB), :],
                                     slab_sc.at[s, pl.ds(j * ROW_SLAB, ROW_SLAB), :], sem.at[s])

    @pl.when(i < n_tiles)
    def _():
        def issue(j, c):
            slab_copy(slot, j, src_ref[0, 0, j]).start()
            return c
        lax.fori_loop(0, tm, issue, 0, unroll=DMA_UNROLL)

    @pl.when(i > 0)
    def _():
        done = 1 - slot

        def wait(j, c):
            slab_copy(done, j, 0).wait()
            return c
        lax.fori_loop(0, tm, wait, 0, unroll=DMA_UNROLL)
        for c in range(ROW_SLAB):
            cs = slice(c * V7X_LANES, (c + 1) * V7X_LANES)
            o_ref[:, cs] = slab_sc[done, pl.ds(c, tm, stride=ROW_SLAB), :].astype(o_ref.dtype)


def _moe_gather(xc, src):
    n_tiles, _, tm = src.shape
    vmem = 2 * tm * D_MODEL * 4 + 2 * tm * D_MODEL * 2 + 2 * tm * D_MODEL * 4 + (8 << 20)
    return pl.pallas_call(
        _moe_gather_kernel, grid=(n_tiles + 1,),
        in_specs=[pl.BlockSpec((1, 1, tm), lambda i: (jnp.minimum(i, n_tiles - 1), 0, 0), memory_space=pltpu.SMEM),
                  pl.BlockSpec(memory_space=pl.ANY)],
        out_specs=pl.BlockSpec((tm, D_MODEL), lambda i: (jnp.maximum(i - 1, 0), 0)),
        out_shape=jax.ShapeDtypeStruct((n_tiles * tm, D_MODEL), BF16),
        scratch_shapes=[pltpu.VMEM((2, tm * ROW_SLAB, V7X_LANES), F32), pltpu.SemaphoreType.DMA((2,))],
        compiler_params=_params(1, vmem), name="moe_gather",
    )(src, xc)


def _moe_ffn_kernel(te_ref, tv_ref, x_ref, wg_ref, wu_ref, wd_ref, o_ref):
    i = pl.program_id(0)
    j = pl.program_id(1)

    @pl.when(j == 0)
    def _():
        o_ref[...] = jnp.zeros(o_ref.shape, F32)

    @pl.when(tv_ref[i] != 0)
    def _():
        _swiglu_accumulate(x_ref[...], wg_ref, wu_ref, wd_ref, o_ref)


def _moe_ffn(xs, tile_expert, tile_valid, wg, wu, wd):
    P = xs.shape[0]
    tm, tf = MOE_TM, MOE_TF
    nj = D_FF_EXPERT // tf

    def jeff(i, j, tv):
        return jnp.where(tv[i] != 0, j, nj - 1)

    grid_spec = pltpu.PrefetchScalarGridSpec(
        num_scalar_prefetch=2, grid=(P // tm, nj),
        in_specs=[pl.BlockSpec((tm, D_MODEL), lambda i, j, te, tv: (i, 0)),
                  pl.BlockSpec((None, D_MODEL, tf), lambda i, j, te, tv: (te[i], 0, jeff(i, j, tv))),
                  pl.BlockSpec((None, D_MODEL, tf), lambda i, j, te, tv: (te[i], 0, jeff(i, j, tv))),
                  pl.BlockSpec((None, tf, D_MODEL), lambda i, j, te, tv: (te[i], jeff(i, j, tv), 0))],
        out_specs=pl.BlockSpec((tm, D_MODEL), lambda i, j, te, tv: (i, 0)))
    vmem = (2 * tm * D_MODEL * 2 + 2 * 3 * D_MODEL * tf * 2 + 2 * tm * D_MODEL * 4
            + 4 * tm * tf * 4 + 2 * tm * D_MODEL * 4 + (8 << 20))
    return pl.pallas_call(
        _moe_ffn_kernel, grid_spec=grid_spec, out_shape=jax.ShapeDtypeStruct((P, D_MODEL), F32),
        compiler_params=_params(2, vmem), name="moe_ffn",
    )(tile_expert, tile_valid, xs, wg, wu, wd)


def _moe_combine_kernel(dest_ref, y_hbm, route_ref, x_ref, g_ref, b_ref, o_ref, ob_ref, y0_sc, y1_sc, sem,
                        *, split_tiles):
    tm = x_ref.shape[0]

    def row_copy(j, k, p):
        dst = y0_sc if k == 0 else y1_sc
        return pltpu.make_async_copy(y_hbm.at[pl.ds(p, 1), :], dst.at[pl.ds(j, 1), :], sem)

    def issue(j, c):
        row_copy(j, 0, dest_ref[0, 0, 2 * j]).start()
        row_copy(j, 1, dest_ref[0, 0, 2 * j + 1]).start()
        return c

    def wait(j, c):
        row_copy(j, 0, 0).wait()
        row_copy(j, 1, 0).wait()
        return c

    lax.fori_loop(0, tm, issue, 0, unroll=DMA_UNROLL)
    lax.fori_loop(0, tm, wait, 0, unroll=DMA_UNROLL)
    route = route_ref[...]
    f = route[:, 2:3] * y0_sc[...] + route[:, 3:4] * y1_sc[...]
    out = _layer_norm_rows(DEEPNORM_ALPHA * x_ref[...] + f, g_ref[...], b_ref[...])
    if split_tiles is None:
        o_ref[...] = out
        ob_ref[...] = out.astype(BF16)
    else:
        m = pl.program_id(0)

        @pl.when(m < split_tiles)
        def _():
            o_ref[...] = out

        @pl.when(m >= split_tiles)
        def _():
            ob_ref[...] = out


def _moe_combine(y, dest, route, x, g, b, split_rows=None):
    T = x.shape[0]
    tm = COMBINE_TM
    row = lambda m: (m, 0)
    const = lambda m: (0, 0)
    if split_rows is None:
        split_tiles = None
        out_specs = [pl.BlockSpec((tm, D_MODEL), row), pl.BlockSpec((tm, D_MODEL), row)]
        out_shape = [jax.ShapeDtypeStruct((T, D_MODEL), F32), jax.ShapeDtypeStruct((T, D_MODEL), BF16)]
    else:
        assert split_rows % tm == 0 and 0 < split_rows < T
        split_tiles = split_rows // tm
        out_specs = [pl.BlockSpec((tm, D_MODEL), lambda m: (jnp.minimum(m, split_tiles - 1), 0)),
                     pl.BlockSpec((tm, D_MODEL), lambda m: (jnp.maximum(m - split_tiles, 0), 0))]
        out_shape = [jax.ShapeDtypeStruct((split_rows, D_MODEL), F32),
                     jax.ShapeDtypeStruct((T - split_rows, D_MODEL), F32)]
    vmem = 2 * tm * D_MODEL * 4 + 2 * tm * D_MODEL * 4 + 2 * tm * D_MODEL * 6 + 6 * tm * D_MODEL * 4 + (8 << 20)
    return pl.pallas_call(
        functools.partial(_moe_combine_kernel, split_tiles=split_tiles), grid=(T // tm,),
        in_specs=[pl.BlockSpec((1, 1, 2 * tm), lambda m: (m, 0, 0), memory_space=pltpu.SMEM),
                  pl.BlockSpec(memory_space=pl.ANY),
                  pl.BlockSpec((tm, V7X_LANES), row),
                  pl.BlockSpec((tm, D_MODEL), row),
                  pl.BlockSpec((1, D_MODEL), const), pl.BlockSpec((1, D_MODEL), const)],
        out_specs=out_specs, out_shape=out_shape,
        scratch_shapes=[pltpu.VMEM((tm, D_MODEL), F32), pltpu.VMEM((tm, D_MODEL), F32),
                        pltpu.SemaphoreType.DMA(())],
        compiler_params=_params(1, vmem), name="moe_combine",
    )(dest.reshape(T // tm, 1, 2 * tm), y, route, x, g, b)


def _route_plan(route, tm):
    T = route.shape[0]
    n_assign = T * TOP_K
    n_tiles = n_assign // tm + N_EXPERTS
    flat_e = route[:, :TOP_K].astype(jnp.int32).reshape(n_assign)
    onehot = (jnp.arange(N_EXPERTS, dtype=jnp.int32)[:, None] == flat_e[None, :]).astype(F32)
    blocks = onehot.reshape(N_EXPERTS, n_assign // V7X_LANES, V7X_LANES)
    tri = (jnp.arange(V7X_LANES)[:, None] >= jnp.arange(V7X_LANES)[None, :]).astype(F32)
    within = jnp.einsum("ebj,ij->ebi", blocks, tri)
    totals = within[:, :, -1]
    running = jnp.cumsum(totals, axis=1)
    csum = (within + (running - totals)[:, :, None]).reshape(N_EXPERTS, n_assign)
    counts = running[:, -1].astype(jnp.int32)
    tiles_per = (counts + tm - 1) // tm
    tile_end = jnp.cumsum(tiles_per)
    row_start = (tile_end - tiles_per) * tm
    dest = jnp.sum(onehot * (csum - 1.0 + row_start.astype(F32)[:, None]), axis=0).astype(jnp.int32)
    tile_ids = jnp.arange(n_tiles, dtype=jnp.int32)
    n_valid = tile_end[-1]
    tile_valid = (tile_ids < n_valid).astype(jnp.int32)
    owner = jnp.sum((jnp.minimum(tile_ids, n_valid - 1)[:, None] >= tile_end[None, :]).astype(jnp.int32), axis=1)
    tile_expert = jnp.minimum(owner, N_EXPERTS - 1).astype(jnp.int32)
    src = jnp.zeros((n_tiles * tm,), jnp.int32).at[dest].set(jnp.arange(n_assign, dtype=jnp.int32) // TOP_K)
    return dest.reshape(T, TOP_K), src.reshape(n_tiles, 1, tm), tile_expert, tile_valid


def _t5_bucket(rel):
    nb = N_BUCKETS // 2
    max_exact = nb // 2
    n = jnp.abs(rel)
    large = max_exact + (jnp.log(jnp.maximum(n, 1).astype(F32) / max_exact)
                         / math.log(MAX_DISTANCE / max_exact) * (nb - max_exact)).astype(jnp.int32)
    large = jnp.minimum(large, nb - 1)
    return jnp.where(rel > 0, nb, 0) + jnp.where(n < max_exact, n, large)


def _a_bias_table(rel_bias, d):
    i = jnp.arange(A_SUB, dtype=jnp.int32)[:, None]
    j = jnp.arange(A_SUB + 2 * A_HALF, dtype=jnp.int32)[None, :]
    rel = j - A_HALF - i
    bucket = _t5_bucket(rel * d)
    hit = bucket[None, :, :, None] == jnp.arange(N_BUCKETS, dtype=jnp.int32)
    bias = jnp.sum(jnp.where(hit, rel_bias.astype(F32).T[:, None, None, :], 0.0), axis=-1)
    return jnp.where((jnp.abs(rel) <= A_HALF)[None], bias, NEG_INF)


def _rope_tables(seqs):
    pos = jnp.concatenate([jnp.arange(sl, dtype=jnp.int32) for _, sl in seqs])
    row = (pos // GRID_W).astype(F32)
    col = (pos % GRID_W).astype(F32)
    n_freq = HEAD_DIM // 4
    inv_freq = ROPE_THETA ** (-jnp.arange(n_freq, dtype=F32) / n_freq)
    ang_r = row[:, None] * inv_freq[None, :]
    ang_c = col[:, None] * inv_freq[None, :]
    ang = jnp.concatenate([ang_r, ang_r, ang_c, ang_c], axis=-1)
    sign = jnp.where((jnp.arange(HEAD_DIM) % (HEAD_DIM // 2)) < (HEAD_DIM // 4), -1.0, 1.0).astype(F32)
    return jnp.cos(ang), jnp.sin(ang) * sign[None, :]


def kernel(x_prompt, x_sample, w_in, b_gate, q_norm_g, k_norm_g, rel_bias, w_branch_a, w_branch_b, w_out,
           ln1_g, ln1_b, ln2_g, ln2_b, ffn_w_gate, ffn_w_up, ffn_w_down, router_w, exp_w_gate, exp_w_up,
           exp_w_down):
    seqs = []
    for xs in (x_prompt, x_sample):
        for _ in range(xs.shape[0]):
            seqs.append((sum(sl for _, sl in seqs), xs.shape[1]))
    seqs = tuple(seqs)
    T = sum(sl for _, sl in seqs)
    assert all(s0 % max(PROJ_TM, B_TQ) == 0 and sl % max(PROJ_TM, B_TQ) == 0 for s0, sl in seqs)

    x = jnp.concatenate([x_prompt.reshape(-1, D_MODEL), x_sample.reshape(-1, D_MODEL)], axis=0)
    xb = x.astype(BF16)
    cos, sin_signed = _rope_tables(seqs)
    biases = [_a_bias_table(rel_bias, d) for _, d in DILATED_PATTERNS]
    router_pad = jnp.pad(router_w.astype(F32), ((0, 0), (0, 0), (0, V7X_LANES - N_EXPERTS)))
    router_hi = router_pad.astype(BF16)
    router_lo = (router_pad - router_hi.astype(F32)).astype(BF16)
    router_split = jnp.concatenate([router_hi, router_lo], axis=-1)
    n_prompt = x_prompt.shape[0] * x_prompt.shape[1]

    for l in range(DEPTH):
        a1, a4, a16, qb, kb, vb, gates = _in_proj(
            xb, w_in[l].astype(BF16), b_gate[l].reshape(1, 2 * D_MODEL),
            q_norm_g[l].reshape(1, HEAD_DIM), k_norm_g[l].reshape(1, HEAD_DIM), cos, sin_signed)
        o1, l1 = _attn_a(a1, biases[0], 1, seqs)
        o4, l4 = _attn_a(a4, biases[1], 4, seqs)
        o16, l16 = _attn_a(a16, biases[2], 16, seqs)
        ob = _attn_b(qb, kb, vb, seqs)
        moe = l % 2 == 1
        res = _branch(o1, o4, o16, l1, l4, l16, ob, gates, x,
                      w_branch_a[l].astype(BF16), w_branch_b[l].astype(BF16), w_out[l].astype(BF16),
                      ln1_g[l].reshape(1, D_MODEL), ln1_b[l].reshape(1, D_MODEL),
                      router_split[l // 2] if moe else None)
        g2, b2 = ln2_g[l].reshape(1, D_MODEL), ln2_b[l].reshape(1, D_MODEL)
        j = l // 2
        if not moe:
            x1, x1b = res
            x, xb = _ffn_dense(x1b, x1, ffn_w_gate[j].astype(BF16), ffn_w_up[j].astype(BF16),
                               ffn_w_down[j].astype(BF16), g2, b2)
        else:
            x1, x1c, route = res
            dest, src, tile_expert, tile_valid = _route_plan(route, MOE_TM)
            xs = _moe_gather(x1c, src)
            ys = _moe_ffn(xs, tile_expert, tile_valid, exp_w_gate[j].astype(BF16), exp_w_up[j].astype(BF16),
                          exp_w_down[j].astype(BF16))
            if l == DEPTH - 1:
                y_prompt, y_sample = _moe_combine(ys, dest, route, x1, g2, b2, split_rows=n_prompt)
                return (y_prompt.reshape(x_prompt.shape), y_sample.reshape(x_sample.shape))
            x, xb = _moe_combine(ys, dest, route, x1, g2, b2)

    return (x[:n_prompt].reshape(x_prompt.shape), x[n_prompt:].reshape(x_sample.shape))
```
